```python
import math
import jax, jax.numpy as jnp
from jax import lax
import numpy as np

D_MODEL = 1024
BATCH = 8
SEQ = 4096
DEPTH = 4

S5_WIDTH = 256
S5_GROUP = 16
S5_GROUPS = S5_WIDTH // S5_GROUP
S5_STATE = 64
S5_DT_MIN = 0.001
S5_DT_MAX = 0.1

DSA_HEADS = 8
DSA_HEAD_DIM = 64
DSA_WIDTH = DSA_HEADS * DSA_HEAD_DIM
DSA_KV_DIM = 64
IDX_HEADS = 4
IDX_DIM = 64
DSA_TOPK_MAX = 256
Q_BLOCK = 128

RET_HEADS = 4
RET_HEAD_DIM = 64
RET_WIDTH = RET_HEADS * RET_HEAD_DIM
RET_CHUNK = 128
ROPE_BASE = 10000.0

REL_BUCKETS = 32
REL_MAX_DIST = 128

D_FF = ((8 * D_MODEL // 3 + 255) // 256) * 256
MIX_WIDTH = S5_WIDTH + DSA_WIDTH + RET_WIDTH

IN_SPLITS = (S5_WIDTH,
             DSA_WIDTH, DSA_KV_DIM, DSA_KV_DIM,
             IDX_HEADS * IDX_DIM, IDX_DIM, IDX_HEADS,
             RET_WIDTH, RET_WIDTH, RET_WIDTH, RET_WIDTH)
IN_WIDTH = sum(IN_SPLITS)

RMS_EPS = 1e-6
GN_EPS = 1e-6
NEG_INF = -1e30

kernel_name = "hybrid_s5_dsa_retention_trunk"


def rmsnorm(x, g):
    xf = x.astype(jnp.float32)
    y = xf * lax.rsqrt(jnp.mean(xf * xf, axis=-1, keepdims=True) + RMS_EPS)
    return (y * g.astype(jnp.float32)).astype(x.dtype)


def t5_bucket(dist):
    n = jnp.maximum(dist, 0)
    max_exact = REL_BUCKETS // 2
    nf = jnp.maximum(n, 1).astype(jnp.float32)
    large = max_exact + (jnp.log(nf / max_exact) / math.log(REL_MAX_DIST / max_exact)
                         * (REL_BUCKETS - max_exact)).astype(jnp.int32)
    large = jnp.minimum(large, REL_BUCKETS - 1)
    return jnp.where(n < max_exact, n, large)


def rotary(x, pos):
    half = x.shape[-1] // 2
    inv = ROPE_BASE ** (-jnp.arange(half, dtype=jnp.float32) / half)
    ang = pos.astype(jnp.float32)[:, None] * inv[None, :]
    cos = jnp.cos(ang)[None, :, None, :]
    sin = jnp.sin(ang)[None, :, None, :]
    x1, x2 = x[..., :half], x[..., half:]
    return jnp.concatenate([x1 * cos - x2 * sin, x2 * cos + x1 * sin], axis=-1)


def s5_mixer(u, a_re, a_im, log_dt, b_re, b_im, c_re, c_im, d_skip, glu_w, glu_b):
    bsz, seq, _ = u.shape
    f32 = jnp.float32
    uf = u.astype(f32).reshape(bsz, seq, S5_GROUPS, S5_GROUP)
    ar, ai = a_re.astype(f32), a_im.astype(f32)
    dt = jnp.exp(log_dt.astype(f32))[:, None]
    mag = jnp.exp(ar * dt)
    ang = ai * dt
    abr, abi = mag * jnp.cos(ang), mag * jnp.sin(ang)
    den = ar * ar + ai * ai
    nr, ni = abr - 1.0, abi
    cr = (nr * ar + ni * ai) / den
    ci = (ni * ar - nr * ai) / den
    br, bi = b_re.astype(f32), b_im.astype(f32)
    bbr = cr[..., None] * br - ci[..., None] * bi
    bbi = cr[..., None] * bi + ci[..., None] * br
    xr = jnp.einsum('blgp,gnp->blgn', uf, bbr)
    xi = jnp.einsum('blgp,gnp->blgn', uf, bbi)
    a_r = jnp.broadcast_to(abr, (1, seq, S5_GROUPS, S5_STATE))
    a_i = jnp.broadcast_to(abi, (1, seq, S5_GROUPS, S5_STATE))

    def combine(e1, e2):
        a1r, a1i, b1r, b1i = e1
        a2r, a2i, b2r, b2i = e2
        return (a2r * a1r - a2i * a1i,
                a2r * a1i + a2i * a1r,
                a2r * b1r - a2i * b1i + b2r,
                a2r * b1i + a2i * b1r + b2i)

    _, _, hr, hi = lax.associative_scan(combine, (a_r, a_i, xr, xi), axis=1)
    y = (jnp.einsum('blgn,gpn->blgp', hr, c_re.astype(f32))
         - jnp.einsum('blgn,gpn->blgp', hi, c_im.astype(f32))
         + d_skip.astype(f32)[None, None] * uf)
    y = jax.nn.gelu(y.reshape(bsz, seq, S5_WIDTH))
    y = y * jax.nn.sigmoid(y @ glu_w.astype(f32) + glu_b.astype(f32))
    return y.astype(u.dtype)


def dsa_attention(q, k, v, q_idx, k_idx, w_idx, rel_bias):
    f32 = jnp.float32
    bsz, seq = q.shape[0], q.shape[1]
    k_sel = min(DSA_TOPK_MAX, seq // 4)
    n_blk = seq // Q_BLOCK

    def to_blocks(a):
        return jnp.swapaxes(a.reshape((bsz, n_blk, Q_BLOCK) + a.shape[2:]), 0, 1)

    kidx = k_idx.astype(f32)
    key_pos = jnp.arange(seq, dtype=jnp.int32)
    scale = DSA_HEAD_DIM ** -0.5

    def block(args):
        qb, qib, wb, start = args
        q_pos = start + jnp.arange(Q_BLOCK, dtype=jnp.int32)
        dots = jnp.einsum('bqhd,bsd->bqhs', qib.astype(f32), kidx) * (IDX_DIM ** -0.5)
        score = jnp.einsum('bqhs,bqh->bqs', jax.nn.relu(dots), wb.astype(f32)) * (IDX_HEADS ** -0.5)
        causal = key_pos[None, None, :] <= q_pos[None, :, None]
        score = jnp.where(causal, score, NEG_INF)
        _, sel = lax.top_k(score, k_sel)
        flat = sel.reshape(bsz, Q_BLOCK * k_sel, 1)
        kg = jnp.take_along_axis(k, flat, axis=1).reshape(bsz, Q_BLOCK, k_sel, DSA_KV_DIM)
        vg = jnp.take_along_axis(v, flat, axis=1).reshape(bsz, Q_BLOCK, k_sel, DSA_KV_DIM)
        dist = q_pos[None, :, None] - sel
        bias = jnp.moveaxis(rel_bias[t5_bucket(dist)], -1, 2)
        logits = (jnp.einsum('bqhd,bqkd->bqhk', qb, kg).astype(f32) * scale
                  + bias.astype(f32))
        logits = jnp.where((dist >= 0)[:, :, None, :], logits, NEG_INF)
        p = jax.nn.softmax(logits, axis=-1).astype(v.dtype)
        return jnp.einsum('bqhk,bqkd->bqhd', p, vg)

    starts = jnp.arange(n_blk, dtype=jnp.int32) * Q_BLOCK
    out = lax.map(block, (to_blocks(q), to_blocks(q_idx), to_blocks(w_idx), starts))
    return jnp.swapaxes(out, 0, 1).reshape(bsz, seq, DSA_WIDTH)


def retention(q, k, v, g, gn_w):
    f32 = jnp.float32
    bsz, seq, _ = q.shape
    nc = seq // RET_CHUNK
    pos = jnp.arange(seq, dtype=jnp.int32)
    shp = (bsz, seq, RET_HEADS, RET_HEAD_DIM)
    qf = rotary(q.astype(f32).reshape(shp), pos)
    kf = rotary(k.astype(f32).reshape(shp), pos) * (RET_HEAD_DIM ** -0.5)
    vf = v.astype(f32).reshape(shp)
    cshp = (bsz, nc, RET_CHUNK, RET_HEADS, RET_HEAD_DIM)
    qc, kc, vc = qf.reshape(cshp), kf.reshape(cshp), vf.reshape(cshp)

    gamma = 1.0 - 2.0 ** (-5.0 - jnp.arange(RET_HEADS, dtype=f32))
    log_g = jnp.log(gamma)
    i = jnp.arange(RET_CHUNK, dtype=f32)
    diff = i[:, None] - i[None, :]
    decay = jnp.where(diff >= 0, jnp.exp(log_g[:, None, None] * jnp.maximum(diff, 0.0)), 0.0)

    qk = jnp.einsum('bnihd,bnjhd->bnhij', qc, kc) * decay[None, None]
    inner = jnp.einsum('bnhij,bnjhd->bnihd', qk, vc)

    wk = jnp.exp(log_g[:, None] * (RET_CHUNK - 1.0 - i)[None, :])
    kv = jnp.einsum('bnjhd,hj,bnjhe->bnhde', kc, wk, vc)
    g_chunk = jnp.exp(log_g * RET_CHUNK)[None, :, None, None]

    def step(state, kv_n):
        return g_chunk * state + kv_n, state

    init = jnp.zeros((bsz, RET_HEADS, RET_HEAD_DIM, RET_HEAD_DIM), f32)
    _, prev = lax.scan(step, init, jnp.swapaxes(kv, 0, 1))
    prev = jnp.swapaxes(prev, 0, 1)
    wq = jnp.exp(log_g[:, None] * (i + 1.0)[None, :])
    cross = jnp.einsum('bnihd,bnhde->bnihe', qc, prev) * wq.T[None, None, :, :, None]
    o = (inner + cross).reshape(shp)

    mu = jnp.mean(o, axis=-1, keepdims=True)
    var = jnp.mean(jnp.square(o - mu), axis=-1, keepdims=True)
    o = ((o - mu) * lax.rsqrt(var + GN_EPS)).reshape(bsz, seq, RET_WIDTH) * gn_w.astype(f32)
    return (jax.nn.silu(g.astype(f32)) * o).astype(q.dtype)


def setup_inputs(seed: int = 0) -> dict:
    key = jax.random.key(seed)
    ks = jax.random.split(key, 24)
    f32 = jnp.float32
    nrm = lambda k, s: jax.random.normal(k, s, f32)
    G, N, P = S5_GROUPS, S5_STATE, S5_GROUP
    lo, hi = math.log(S5_DT_MIN), math.log(S5_DT_MAX)
    return {
        "x": nrm(ks[0], (BATCH, SEQ, D_MODEL)),
        "w_in": nrm(ks[1], (DEPTH, D_MODEL, IN_WIDTH)) * D_MODEL ** -0.5,
        "w_out": nrm(ks[2], (DEPTH, MIX_WIDTH, D_MODEL)) * MIX_WIDTH ** -0.5,
        "norm_mix": 1.0 + 0.02 * nrm(ks[3], (DEPTH, D_MODEL)),
        "ssm_a_re": -0.5 + 0.01 * nrm(ks[4], (DEPTH, G, N)),
        "ssm_a_im": math.pi * jnp.arange(N, dtype=f32)[None, None, :] + 0.01 * nrm(ks[5], (DEPTH, G, N)),
        "ssm_log_dt": lo + (hi - lo) * jax.random.uniform(ks[6], (DEPTH, G), f32),
        "ssm_b_re": nrm(ks[7], (DEPTH, G, N, P)) * (2.0 * P) ** -0.5,
        "ssm_b_im": nrm(ks[8], (DEPTH, G, N, P)) * (2.0 * P) ** -0.5,
        "ssm_c_re": nrm(ks[9], (DEPTH, G, P, N)) * (2.0 * N) ** -0.5,
        "ssm_c_im": nrm(ks[10], (DEPTH, G, P, N)) * (2.0 * N) ** -0.5,
        "ssm_d": nrm(ks[11], (DEPTH, G, P)),
        "ssm_glu_w": nrm(ks[12], (DEPTH, S5_WIDTH, S5_WIDTH)) * S5_WIDTH ** -0.5,
        "ssm_glu_b": 0.01 * nrm(ks[13], (DEPTH, S5_WIDTH)),
        "rel_bias": 0.5 * nrm(ks[14], (REL_BUCKETS, DSA_HEADS)),
        "ret_gn": 1.0 + 0.02 * nrm(ks[15], (DEPTH, RET_WIDTH)),
        "norm_ffn": 1.0 + 0.02 * nrm(ks[16], (DEPTH, D_MODEL)),
        "w_ffn_in": nrm(ks[17], (DEPTH, D_MODEL, 2 * D_FF)) * D_MODEL ** -0.5,
        "w_ffn_out": nrm(ks[18], (DEPTH, D_FF, D_MODEL)) * D_FF ** -0.5,
        "norm_final": 1.0 + 0.02 * nrm(ks[19], (D_MODEL,)),
    }


def reference(x, w_in, w_out, norm_mix, ssm_a_re, ssm_a_im, ssm_log_dt, ssm_b_re, ssm_b_im,
              ssm_c_re, ssm_c_im, ssm_d, ssm_glu_w, ssm_glu_b, rel_bias, ret_gn,
              norm_ffn, w_ffn_in, w_ffn_out, norm_final):
    bsz, seq, _ = x.shape
    split_pts = [int(s) for s in np.cumsum(IN_SPLITS)[:-1]]
    for l in range(DEPTH):
        h = rmsnorm(x, norm_mix[l])
        proj = h @ w_in[l]
        (u, dq, dk, dv, iq, ik, iw, rq, rk, rv, rg) = jnp.split(proj, split_pts, axis=-1)
        y_a = s5_mixer(u, ssm_a_re[l], ssm_a_im[l], ssm_log_dt[l], ssm_b_re[l], ssm_b_im[l],
                       ssm_c_re[l], ssm_c_im[l], ssm_d[l], ssm_glu_w[l], ssm_glu_b[l])
        y_b = dsa_attention(dq.reshape(bsz, seq, DSA_HEADS, DSA_HEAD_DIM), dk, dv,
                            iq.reshape(bsz, seq, IDX_HEADS, IDX_DIM), ik, iw, rel_bias)
        y_c = retention(rq, rk, rv, rg, ret_gn[l])
        mix = jnp.concatenate([y_a, y_b.astype(x.dtype), y_c], axis=-1)
        x = x + mix @ w_out[l]
        h = rmsnorm(x, norm_ffn[l])
        gate, up = jnp.split(h @ w_ffn_in[l], 2, axis=-1)
        x = x + (jax.nn.silu(gate) * up) @ w_ffn_out[l]
    return rmsnorm(x, norm_final)
```

```python
import functools
import math

import numpy as np
import jax
import jax.numpy as jnp
from jax import lax
from jax.experimental import pallas as pl
from jax.experimental.pallas import tpu as pltpu

F32 = jnp.float32
BF16 = jnp.bfloat16
I32 = jnp.int32

D_MODEL = 1024
DEPTH = 4
S5_WIDTH = 256
S5_GROUP = 16
S5_GROUPS = 16
S5_STATE = 64
DSA_HEADS = 8
DSA_HEAD_DIM = 64
DSA_WIDTH = 512
DSA_KV_DIM = 64
IDX_HEADS = 4
IDX_DIM = 64
DSA_TOPK_MAX = 256
RET_HEADS = 4
RET_HEAD_DIM = 64
RET_WIDTH = 256
ROPE_BASE = 10000.0
REL_BUCKETS = 32
REL_MAX_DIST = 128
D_FF = 2816
IN_SPLITS = (256, 512, 64, 64, 256, 64, 4, 256, 256, 256, 256)
RMS_EPS = 1e-6
GN_EPS = 1e-6

COL_DQ = 0
COL_IQ = 512
COL_RQ = 768
COL_RK = 1024
COL_RV = 1280
COL_RG = 1536
COL_KG = 1792
COL_VG = 1920
COL_IX = 2048
MAIN_W = 2304
PROJ_W = MAIN_W + S5_WIDTH
PROJ_CHUNK = 256

LANES = 128
Q_BLK = 128
K_BLK = 256
NEG_BIG = -1e30
INT_MIN = -2147483648
KEY_NEG_INF = -2139095041
FF_CHUNK = 256
N_FF_CHUNK = D_FF // FF_CHUNK
RET_CHUNK = 256
S5_TL = 64
TM_PROJ = 512
TM_FFN = 512
VMEM_LIMIT = 56 * 1024 * 1024


def _nt_dot(a, b):
    return lax.dot_general(a, b, (((1,), (1,)), ((), ())), preferred_element_type=F32)


def _dot(a, b):
    return jnp.dot(a, b, preferred_element_type=F32)


def _inproj_kernel(x_ref, g_ref, w_ref, main_ref, idx_ref, u_ref):
    x = x_ref[0]
    ms = jnp.mean(x * x, axis=-1, keepdims=True)
    h = (x * lax.rsqrt(ms + RMS_EPS) * g_ref[...]).astype(BF16)
    n_main = MAIN_W // PROJ_CHUNK
    for c in range(n_main + 1):
        acc = _dot(h, w_ref[:, c * PROJ_CHUNK:(c + 1) * PROJ_CHUNK])
        if c == n_main:
            u_ref[...] = acc
            continue
        if c * PROJ_CHUNK <= COL_VG < (c + 1) * PROJ_CHUNK:
            lane = lax.broadcasted_iota(I32, acc.shape, 1)
            acc = jnp.where(lane == COL_VG - c * PROJ_CHUNK + DSA_KV_DIM, 1.0, acc)
        if c * PROJ_CHUNK == COL_IX:
            idx_ref[0] = acc[:, :LANES]
        main_ref[0, :, c * PROJ_CHUNK:(c + 1) * PROJ_CHUNK] = acc.astype(BF16)


def _inproj(x, g, w_all):
    bsz, seq, _ = x.shape
    tm = min(TM_PROJ, seq)
    return pl.pallas_call(
        _inproj_kernel,
        grid=(bsz, seq // tm),
        in_specs=[
            pl.BlockSpec((1, tm, D_MODEL), lambda b, i: (b, i, 0)),
            pl.BlockSpec((1, D_MODEL), lambda b, i: (0, 0)),
            pl.BlockSpec((D_MODEL, PROJ_W), lambda b, i: (0, 0)),
        ],
        out_specs=[
            pl.BlockSpec((1, tm, MAIN_W), lambda b, i: (b, i, 0)),
            pl.BlockSpec((1, tm, LANES), lambda b, i: (b, i, 0)),
            pl.BlockSpec((tm, S5_WIDTH), lambda b, i: (i, b)),
        ],
        out_shape=[
            jax.ShapeDtypeStruct((bsz, seq, MAIN_W), BF16),
            jax.ShapeDtypeStruct((bsz, seq, LANES), F32),
            jax.ShapeDtypeStruct((seq, bsz * S5_WIDTH), F32),
        ],
        compiler_params=pltpu.CompilerParams(
            dimension_semantics=("arbitrary", "arbitrary"), vmem_limit_bytes=VMEM_LIMIT),
        name="inproj",
    )(x, g, w_all)


def _s5_kernel(u_ref, bmat_ref, cmat_ref, ar_ref, ai_ref, d_ref, gw_ref, gb_ref, o_ref,
               xs_ref, h_ref):
    tl, bsz, _ = u_ref.shape
    nst = S5_GROUPS * S5_STATE

    @pl.when(pl.program_id(0) == 0)
    def _():
        h_ref[...] = jnp.zeros_like(h_ref)

    u = u_ref[...].reshape(tl * bsz, S5_WIDTH)
    xs_ref[...] = _dot(u.astype(BF16), bmat_ref[...])
    ar = jnp.broadcast_to(ar_ref[...], (bsz, nst))
    ai = jnp.broadcast_to(ai_ref[...], (bsz, nst))

    def step(t, carry):
        hr, hi = carry
        rows = pl.ds(pl.multiple_of(t * bsz, bsz), bsz)
        nr = ar * hr - ai * hi + xs_ref[rows, 0:nst]
        ni = ar * hi + ai * hr + xs_ref[rows, nst:2 * nst]
        xs_ref[rows, 0:nst] = nr
        xs_ref[rows, nst:2 * nst] = ni
        return nr, ni

    hr, hi = lax.fori_loop(0, tl, step, (h_ref[:, 0:nst], h_ref[:, nst:2 * nst]), unroll=4)
    h_ref[:, 0:nst] = hr
    h_ref[:, nst:2 * nst] = hi

    y = _dot(xs_ref[...].astype(BF16), cmat_ref[...]) + d_ref[...] * u
    y = jax.nn.gelu(y)
    z = _dot(y.astype(BF16), gw_ref[...]) + gb_ref[...]
    o_ref[...] = (y * jax.nn.sigmoid(z)).reshape(tl, bsz, S5_WIDTH)


def _s5(u_t, bmat, cmat, ar, ai, dvec, gw, gb):
    seq, bsz, _ = u_t.shape
    tl = min(S5_TL, seq)
    nst2 = 2 * S5_GROUPS * S5_STATE
    const = lambda shape: pl.BlockSpec(shape, lambda i: (0,) * len(shape))
    return pl.pallas_call(
        _s5_kernel,
        grid=(seq // tl,),
        in_specs=[
            pl.BlockSpec((tl, bsz, S5_WIDTH), lambda i: (i, 0, 0)),
            const((S5_WIDTH, nst2)), const((nst2, S5_WIDTH)),
            const((1, nst2 // 2)), const((1, nst2 // 2)), const((1, S5_WIDTH)),
            const((S5_WIDTH, S5_WIDTH)), const((1, S5_WIDTH)),
        ],
        out_specs=pl.BlockSpec((tl, bsz, S5_WIDTH), lambda i: (i, 0, 0)),
        out_shape=jax.ShapeDtypeStruct((seq, bsz, S5_WIDTH), F32),
        scratch_shapes=[pltpu.VMEM((tl * bsz, nst2), F32), pltpu.VMEM((bsz, nst2), F32)],
        compiler_params=pltpu.CompilerParams(
            dimension_semantics=("arbitrary",), vmem_limit_bytes=VMEM_LIMIT),
        name="s5_scan",
    )(u_t, bmat, cmat, ar, ai, dvec, gw, gb)


def _bias_kernel(rb_ref, bk_ref, o_ref):
    h = pl.program_id(1)
    bk = bk_ref[0]
    acc = jnp.zeros(bk.shape, F32)
    for b in range(REL_BUCKETS):
        acc = jnp.where(bk == b, rb_ref[b, h], acc)
    o_ref[0] = acc


def _bucket_tiles():
    d = np.arange(4, dtype=np.int64)[:, None, None] * Q_BLK
    i = np.arange(Q_BLK, dtype=np.int64)[None, :, None]
    j = np.arange(K_BLK, dtype=np.int64)[None, None, :]
    n = np.maximum(d + i - j, 0)
    max_exact = REL_BUCKETS // 2
    nf = np.maximum(n, 1).astype(np.float32)
    large = max_exact + (np.log(nf / np.float32(max_exact))
                         / np.float32(math.log(REL_MAX_DIST / max_exact))
                         * np.float32(REL_BUCKETS - max_exact)).astype(np.int32)
    large = np.minimum(large, REL_BUCKETS - 1)
    return np.where(n < max_exact, n, large).astype(np.int32)


def _bias_tiles(rel_bias):
    buckets = jnp.asarray(_bucket_tiles())
    return pl.pallas_call(
        _bias_kernel,
        grid=(4, DSA_HEADS),
        in_specs=[
            pl.BlockSpec(memory_space=pltpu.SMEM),
            pl.BlockSpec((1, Q_BLK, K_BLK), lambda d, h: (d, 0, 0)),
        ],
        out_specs=pl.BlockSpec((1, Q_BLK, K_BLK), lambda d, h: (d, h, 0)),
        out_shape=jax.ShapeDtypeStruct((4, DSA_HEADS * Q_BLK, K_BLK), F32),
        name="rel_bias_tiles",
    )(rel_bias, buckets)


def _dsa_kernel(dq_ref, iq_ref, iw_ref, ix_ref, kg_ref, vg_ref, bias_ref, tri_ref, y_ref,
                keys_s, wb_s, qs_s, iqs_s, t_s, r_s, m_s, acc_s, er_s, *, k_sel):
    qb = pl.program_id(1)
    nkb = qb // 2 + 1
    row0 = qb * Q_BLK

    for h in range(DSA_HEADS):
        qs_s[h * Q_BLK:(h + 1) * Q_BLK, :] = (
            dq_ref[0, :, h * DSA_HEAD_DIM:(h + 1) * DSA_HEAD_DIM] * jnp.asarray(0.125, BF16))
    w = iw_ref[0]
    for h in range(IDX_HEADS):
        iqs_s[h * Q_BLK:(h + 1) * Q_BLK, :] = iq_ref[0, :, h * IDX_DIM:(h + 1) * IDX_DIM]
        wb_s[h * Q_BLK:(h + 1) * Q_BLK, :] = jnp.broadcast_to(
            w[:, IDX_DIM + h:IDX_DIM + h + 1] * 0.0625, (Q_BLK, K_BLK))

    rowi = row0 + lax.broadcasted_iota(I32, (Q_BLK, K_BLK), 0)
    coli = lax.broadcasted_iota(I32, (Q_BLK, K_BLK), 1)

    def score_block(kb, carry):
        ks = pl.multiple_of(kb * K_BLK, K_BLK)
        ik = ix_ref[0, pl.ds(ks, K_BLK), 0:IDX_DIM]
        r = jnp.maximum(_nt_dot(iqs_s[...], ik), 0.0) * wb_s[...]
        s = (r[0:Q_BLK] + r[Q_BLK:2 * Q_BLK]) + (r[2 * Q_BLK:3 * Q_BLK] + r[3 * Q_BLK:4 * Q_BLK])
        s = jnp.where(s == 0.0, 0.0, s)
        s = jnp.where(coli + ks <= rowi, s, -jnp.inf)
        bits = lax.bitcast_convert_type(s, I32)
        keys_s[:, pl.ds(ks, K_BLK)] = bits ^ ((bits >> 31) & 0x7FFFFFFF)
        return carry

    lax.fori_loop(0, nkb, score_block, 0)

    t_s[...] = jnp.full(t_s.shape, KEY_NEG_INF, I32)
    r_s[...] = jnp.zeros_like(r_s)

    @pl.when(row0 >= k_sel)
    def _():
        ones = jnp.ones((LANES, LANES), BF16)

        def count(cand, strict):
            def body(kb, acc):
                ks = pl.multiple_of(kb * K_BLK, K_BLK)
                blk = keys_s[:, pl.ds(ks, K_BLK)]
                lo, hi = blk[:, :LANES], blk[:, LANES:]
                if strict:
                    return acc + jnp.where(lo > cand, 1.0, 0.0) + jnp.where(hi > cand, 1.0, 0.0)
                return acc + jnp.where(lo >= cand, 1.0, 0.0) + jnp.where(hi >= cand, 1.0, 0.0)
            acc = lax.fori_loop(0, nkb, body, jnp.zeros((Q_BLK, LANES), F32))
            return _dot(acc.astype(BF16), ones)

        def bit_step(i, tu):
            cand_u = tu | lax.shift_left(jnp.int32(1), 31 - i)
            cnt = count(cand_u ^ INT_MIN, False)
            return jnp.where(cnt >= k_sel, cand_u, tu)

        tk = lax.fori_loop(0, 32, bit_step, jnp.zeros((Q_BLK, LANES), I32)) ^ INT_MIN
        t_s[...] = tk
        r_s[...] = k_sel - count(tk, True)

    m_s[...] = jnp.full(m_s.shape, NEG_BIG, F32)
    acc_s[...] = jnp.zeros_like(acc_s)
    er_s[...] = jnp.zeros_like(er_s)
    tk2 = jnp.concatenate([t_s[...], t_s[...]], axis=1)
    rr2 = jnp.concatenate([r_s[...], r_s[...]], axis=1)

    def attn_block(kb, carry):
        ks = pl.multiple_of(kb * K_BLK, K_BLK)
        key = keys_s[:, pl.ds(ks, K_BLK)]
        eq = key == tk2
        pre = _dot(jnp.where(eq, 1.0, 0.0).astype(BF16), tri_ref[...])
        er = er_s[...]
        take = pre[:, :K_BLK] + jnp.concatenate([er, er], axis=1) <= rr2
        pen = jnp.where(key > tk2, 0.0, jnp.where(eq, jnp.where(take, 0.0, NEG_BIG), NEG_BIG))
        er_s[...] = er + pre[:, K_BLK:]

        s = _nt_dot(qs_s[...], kg_ref[0, pl.ds(ks, K_BLK), 0:DSA_KV_DIM])
        s = s + bias_ref[jnp.minimum(qb - 2 * kb, 3)] + jnp.concatenate([pen] * DSA_HEADS, axis=0)
        m_old = m_s[...]
        m_new = jnp.maximum(m_old, jnp.max(s, axis=1, keepdims=True))
        p = jnp.exp(s - jnp.concatenate([m_new, m_new], axis=1))
        pv = _dot(p.astype(BF16), vg_ref[0, pl.ds(ks, K_BLK), :])
        acc_s[...] = jnp.exp(m_old - m_new) * acc_s[...] + pv
        m_s[...] = m_new
        return carry

    lax.fori_loop(0, nkb, attn_block, 0)

    acc = acc_s[...]
    o = acc[:, 0:DSA_HEAD_DIM] / acc[:, DSA_HEAD_DIM:DSA_HEAD_DIM + 1]
    for h in range(DSA_HEADS):
        y_ref[0, :, h * DSA_HEAD_DIM:(h + 1) * DSA_HEAD_DIM] = (
            o[h * Q_BLK:(h + 1) * Q_BLK].astype(BF16))


def _dsa(main, idxf, bias, tri):
    bsz, seq, _ = main.shape
    k_sel = min(DSA_TOPK_MAX, seq // 4)
    assert seq % K_BLK == 0 and k_sel % Q_BLK == 0
    col = lambda c, w: c // w
    return pl.pallas_call(
        functools.partial(_dsa_kernel, k_sel=k_sel),
        grid=(bsz, seq // Q_BLK),
        in_specs=[
            pl.BlockSpec((1, Q_BLK, DSA_WIDTH), lambda b, q: (b, q, col(COL_DQ, DSA_WIDTH))),
            pl.BlockSpec((1, Q_BLK, 256), lambda b, q: (b, q, col(COL_IQ, 256))),
            pl.BlockSpec((1, Q_BLK, LANES), lambda b, q: (b, q, 0)),
            pl.BlockSpec((1, seq, LANES), lambda b, q: (b, 0, col(COL_IX, LANES))),
            pl.BlockSpec((1, seq, LANES), lambda b, q: (b, 0, col(COL_KG, LANES))),
            pl.BlockSpec((1, seq, LANES), lambda b, q: (b, 0, col(COL_VG, LANES))),
            pl.BlockSpec((4, DSA_HEADS * Q_BLK, K_BLK), lambda b, q: (0, 0, 0)),
            pl.BlockSpec((K_BLK, K_BLK + LANES), lambda b, q: (0, 0)),
        ],
        out_specs=pl.BlockSpec((1, Q_BLK, DSA_WIDTH), lambda b, q: (b, q, 0)),
        out_shape=jax.ShapeDtypeStruct((bsz, seq, DSA_WIDTH), BF16),
        scratch_shapes=[
            pltpu.VMEM((Q_BLK, seq), I32),
            pltpu.VMEM((IDX_HEADS * Q_BLK, K_BLK), F32),
            pltpu.VMEM((DSA_HEADS * Q_BLK, DSA_HEAD_DIM), BF16),
            pltpu.VMEM((IDX_HEADS * Q_BLK, IDX_DIM), BF16),
            pltpu.VMEM((Q_BLK, LANES), I32),
            pltpu.VMEM((Q_BLK, LANES), F32),
            pltpu.VMEM((DSA_HEADS * Q_BLK, LANES), F32),
            pltpu.VMEM((DSA_HEADS * Q_BLK, LANES), F32),
            pltpu.VMEM((Q_BLK, LANES), F32),
        ],
        compiler_params=pltpu.CompilerParams(
            dimension_semantics=("arbitrary", "arbitrary"), vmem_limit_bytes=VMEM_LIMIT),
        name="dsa_attention",
    )(main, main, idxf, main, main, main, bias, tri)


def _ret_kernel(q_ref, k_ref, v_ref, g_ref, cos_ref, sa_ref, sb_ref, dec_ref, wq_ref, wk_ref,
                gc_ref, gn_ref, o_ref, st_ref):
    @pl.when(pl.program_id(1) == 0)
    def _():
        st_ref[...] = jnp.zeros_like(st_ref)

    cos, sa, sb = cos_ref[...], sa_ref[...], sb_ref[...]
    half = RET_HEAD_DIM // 2

    def rot(x):
        return (x * cos + pltpu.roll(x, RET_WIDTH - half, 1) * sa + pltpu.roll(x, half, 1) * sb)

    q = rot(q_ref[0].astype(F32))
    k = rot(k_ref[0].astype(F32)) * (RET_HEAD_DIM ** -0.5)
    v = v_ref[0]
    g = g_ref[0].astype(F32)
    gate = g * jax.nn.sigmoid(g)
    gn = gn_ref[...]
    for h in range(RET_HEADS):
        sl = slice(h * RET_HEAD_DIM, (h + 1) * RET_HEAD_DIM)
        qh = q[:, sl].astype(BF16)
        kf = k[:, sl]
        vh = v[:, sl]
        s = _nt_dot(qh, kf.astype(BF16)) * dec_ref[h]
        st = st_ref[h]
        o = _dot(s.astype(BF16), vh) + _dot(qh, st.astype(BF16)) * wq_ref[h]
        kw_t = (kf * wk_ref[h]).T.astype(BF16)
        st_ref[h] = gc_ref[h] * st + _dot(kw_t, vh)
        mu = jnp.mean(o, axis=-1, keepdims=True)
        d = o - mu
        var = jnp.mean(d * d, axis=-1, keepdims=True)
        on = d * lax.rsqrt(var + GN_EPS)
        o_ref[0, :, sl] = (gate[:, sl] * (on * gn[:, sl])).astype(BF16)


def _ret_tables(seq, chunk):
    half = RET_HEAD_DIM // 2
    inv = ROPE_BASE ** (-jnp.arange(half, dtype=F32) / half)
    ang = jnp.arange(seq, dtype=jnp.int32).astype(F32)[:, None] * inv[None, :]
    cos, sin = jnp.cos(ang), jnp.sin(ang)
    zero = jnp.zeros_like(sin)
    cos_t = jnp.tile(cos, (1, 2 * RET_HEADS))
    sa = jnp.tile(jnp.concatenate([-sin, zero], axis=1), (1, RET_HEADS))
    sb = jnp.tile(jnp.concatenate([zero, sin], axis=1), (1, RET_HEADS))
    gamma = 1.0 - 2.0 ** (-5.0 - jnp.arange(RET_HEADS, dtype=F32))
    log_g = jnp.log(gamma)
    i = jnp.arange(chunk, dtype=F32)
    diff = i[:, None] - i[None, :]
    decay = jnp.where(diff >= 0, jnp.exp(log_g[:, None, None] * jnp.maximum(diff, 0.0)), 0.0)
    ones = jnp.ones((1, 1, RET_HEAD_DIM), F32)
    wq = jnp.exp(log_g[:, None] * (i + 1.0)[None, :])[:, :, None] * ones
    wk = jnp.exp(log_g[:, None] * (chunk - 1.0 - i)[None, :])[:, :, None] * ones
    gc = jnp.exp(log_g * chunk)[:, None, None] * jnp.ones((1, RET_HEAD_DIM, RET_HEAD_DIM), F32)
    return cos_t, sa, sb, decay, wq, wk, gc


def _retention(main, tables, gn):
    bsz, seq, _ = main.shape
    cos_t, sa, sb, decay, wq, wk, gc = tables
    c = decay.shape[-1]
    blk = lambda col: pl.BlockSpec((1, c, RET_WIDTH), lambda b, i: (b, i, col // RET_WIDTH))
    tab = pl.BlockSpec((c, RET_WIDTH), lambda b, i: (i, 0))
    const = lambda shape: pl.BlockSpec(shape, lambda b, i: (0,) * len(shape))
    return pl.pallas_call(
        _ret_kernel,
        grid=(bsz, seq // c),
        in_specs=[
            blk(COL_RQ), blk(COL_RK), blk(COL_RV), blk(COL_RG), tab, tab, tab,
            const((RET_HEADS, c, c)), const((RET_HEADS, c, RET_HEAD_DIM)),
            const((RET_HEADS, c, RET_HEAD_DIM)),
            const((RET_HEADS, RET_HEAD_DIM, RET_HEAD_DIM)), const((1, RET_WIDTH)),
        ],
        out_specs=pl.BlockSpec((1, c, RET_WIDTH), lambda b, i: (b, i, 0)),
        out_shape=jax.ShapeDtypeStruct((bsz, seq, RET_WIDTH), BF16),
        scratch_shapes=[pltpu.VMEM((RET_HEADS, RET_HEAD_DIM, RET_HEAD_DIM), F32)],
        compiler_params=pltpu.CompilerParams(
            dimension_semantics=("arbitrary", "arbitrary"), vmem_limit_bytes=VMEM_LIMIT),
        name="retention",
    )(main, main, main, main, cos_t, sa, sb, decay, wq, wk, gc, gn)


def _ffn_kernel(x_ref, ya_ref, yb_ref, yc_ref, wo_ref, g_ref, wg_ref, wu_ref, wd_ref, gf_ref,
                o_ref, x1_s, h_s, acc_s, *, final):
    x1 = (x_ref[0]
          + _dot(ya_ref[...].astype(BF16), wo_ref[0:S5_WIDTH, :])
          + _dot(yb_ref[0], wo_ref[S5_WIDTH:S5_WIDTH + DSA_WIDTH, :])
          + _dot(yc_ref[0], wo_ref[S5_WIDTH + DSA_WIDTH:, :]))
    x1_s[...] = x1
    ms = jnp.mean(x1 * x1, axis=-1, keepdims=True)
    h_s[...] = (x1 * lax.rsqrt(ms + RMS_EPS) * g_ref[...]).astype(BF16)
    acc_s[...] = jnp.zeros_like(acc_s)

    def chunk(c, carry):
        h = h_s[...]
        gate = _dot(h, wg_ref[c])
        up = _dot(h, wu_ref[c])
        a = (gate * jax.nn.sigmoid(gate) * up).astype(BF16)
        acc_s[...] += _dot(a, wd_ref[c])
        return carry

    lax.fori_loop(0, N_FF_CHUNK, chunk, 0)
    y = x1_s[...] + acc_s[...]
    if final:
        ms = jnp.mean(y * y, axis=-1, keepdims=True)
        y = y * lax.rsqrt(ms + RMS_EPS) * gf_ref[...]
    o_ref[0] = y


def _ffn(x, ya, yb, yc, wo, g, wg, wu, wd, gf, final):
    bsz, seq, _ = x.shape
    tm = min(TM_FFN, seq)
    once = pl.Buffered(1)
    const = lambda shape: pl.BlockSpec(shape, lambda b, i: (0,) * len(shape), pipeline_mode=once)
    return pl.pallas_call(
        functools.partial(_ffn_kernel, final=final),
        grid=(bsz, seq // tm),
        in_specs=[
            pl.BlockSpec((1, tm, D_MODEL), lambda b, i: (b, i, 0)),
            pl.BlockSpec((tm, S5_WIDTH), lambda b, i: (i, b)),
            pl.BlockSpec((1, tm, DSA_WIDTH), lambda b, i: (b, i, 0)),
            pl.BlockSpec((1, tm, RET_WIDTH), lambda b, i: (b, i, 0)),
            const((D_MODEL, D_MODEL)), const((1, D_MODEL)),
            const((N_FF_CHUNK, D_MODEL, FF_CHUNK)), const((N_FF_CHUNK, D_MODEL, FF_CHUNK)),
            const((N_FF_CHUNK, FF_CHUNK, D_MODEL)), const((1, D_MODEL)),
        ],
        out_specs=pl.BlockSpec((1, tm, D_MODEL), lambda b, i: (b, i, 0)),
        out_shape=jax.ShapeDtypeStruct((bsz, seq, D_MODEL), F32),
        scratch_shapes=[pltpu.VMEM((tm, D_MODEL), F32), pltpu.VMEM((tm, D_MODEL), BF16),
                        pltpu.VMEM((tm, D_MODEL), F32)],
        compiler_params=pltpu.CompilerParams(
            dimension_semantics=("arbitrary", "arbitrary"), vmem_limit_bytes=VMEM_LIMIT),
        name="outproj_ffn",
    )(x, ya, yb, yc, wo, g, wg, wu, wd, gf)


def _pack_w_in(w_in):
    pts = [int(s) for s in np.cumsum(IN_SPLITS)[:-1]]
    u, dq, dk, dv, iq, ik, iw, rq, rk, rv, rg = jnp.split(w_in, pts, axis=-1)
    z = lambda n: jnp.zeros(w_in.shape[:-1] + (n,), w_in.dtype)
    parts = [dq, iq, rq, rk, rv, rg, dk, z(64), dv, z(64), ik, iw, z(60), z(128), u]
    return jnp.concatenate(parts, axis=-1).astype(BF16)


def _s5_params(a_re, a_im, log_dt, b_re, b_im, c_re, c_im, d_skip):
    g, n, p = S5_GROUPS, S5_STATE, S5_GROUP
    dt = jnp.exp(log_dt)[:, None]
    mag = jnp.exp(a_re * dt)
    ang = a_im * dt
    abr, abi = mag * jnp.cos(ang), mag * jnp.sin(ang)
    den = a_re * a_re + a_im * a_im
    nr, ni = abr - 1.0, abi
    cr = (nr * a_re + ni * a_im) / den
    ci = (ni * a_re - nr * a_im) / den
    bbr = cr[..., None] * b_re - ci[..., None] * b_im
    bbi = cr[..., None] * b_im + ci[..., None] * b_re
    eye = jnp.eye(g, dtype=F32)

    def b_block(bb):
        return (jnp.swapaxes(bb, 1, 2)[:, :, None, :] * eye[:, None, :, None]).reshape(g * p, g * n)

    def c_block(cc):
        return (jnp.swapaxes(cc, 1, 2)[:, :, None, :] * eye[:, None, :, None]).reshape(g * n, g * p)

    bmat = jnp.concatenate([b_block(bbr), b_block(bbi)], axis=1).astype(BF16)
    cmat = jnp.concatenate([c_block(c_re), -c_block(c_im)], axis=0).astype(BF16)
    return bmat, cmat, abr.reshape(1, g * n), abi.reshape(1, g * n), d_skip.reshape(1, g * p)


def _tri_ones():
    j = np.arange(K_BLK)
    tri = (j[:, None] <= j[None, :]).astype(np.float32)
    return jnp.asarray(np.concatenate([tri, np.ones((K_BLK, LANES), np.float32)], axis=1), BF16)


def kernel(x, w_in, w_out, norm_mix, ssm_a_re, ssm_a_im, ssm_log_dt, ssm_b_re, ssm_b_im,
           ssm_c_re, ssm_c_im, ssm_d, ssm_glu_w, ssm_glu_b, rel_bias, ret_gn, norm_ffn,
           w_ffn_in, w_ffn_out, norm_final):
    bsz, seq, _ = x.shape
    depth = w_in.shape[0]
    w_all = _pack_w_in(w_in)
    wo = w_out.astype(BF16)
    wg = w_ffn_in[:, :, :D_FF].reshape(depth, D_MODEL, N_FF_CHUNK, FF_CHUNK)
    wu = w_ffn_in[:, :, D_FF:].reshape(depth, D_MODEL, N_FF_CHUNK, FF_CHUNK)
    wg = jnp.swapaxes(wg, 1, 2).astype(BF16)
    wu = jnp.swapaxes(wu, 1, 2).astype(BF16)
    wd = w_ffn_out.reshape(depth, N_FF_CHUNK, FF_CHUNK, D_MODEL).astype(BF16)
    glu_w = ssm_glu_w.astype(BF16)

    bias = _bias_tiles(rel_bias)
    tri = _tri_ones()
    tables = _ret_tables(seq, min(RET_CHUNK, seq))

    for l in range(depth):
        main, idxf, u_t = _inproj(x, norm_mix[l][None, :], w_all[l])
        s5p = _s5_params(ssm_a_re[l], ssm_a_im[l], ssm_log_dt[l], ssm_b_re[l], ssm_b_im[l],
                         ssm_c_re[l], ssm_c_im[l], ssm_d[l])
        ya = _s5(u_t.reshape(seq, bsz, S5_WIDTH), *s5p, glu_w[l], ssm_glu_b[l][None, :])
        yb = _dsa(main, idxf, bias, tri)
        yc = _retention(main, tables, ret_gn[l][None, :])
        x = _ffn(x, ya.reshape(seq, bsz * S5_WIDTH), yb, yc, wo[l], norm_ffn[l][None, :],
                 wg[l], wu[l], wd[l], norm_final[None, :], final=(l == depth - 1))
    return x
```

```python
import functools
import math

import numpy as np
import jax
import jax.numpy as jnp
from jax import lax
from jax.experimental import pallas as pl
from jax.experimental.pallas import tpu as pltpu

F32 = jnp.float32
BF16 = jnp.bfloat16
I32 = jnp.int32

D_MODEL = 1024
DEPTH = 4
S5_WIDTH = 256
S5_GROUP = 16
S5_GROUPS = 16
S5_STATE = 64
DSA_HEADS = 8
DSA_HEAD_DIM = 64
DSA_WIDTH = 512
DSA_KV_DIM = 64
IDX_HEADS = 4
IDX_DIM = 64
DSA_TOPK_MAX = 256
RET_HEADS = 4
RET_HEAD_DIM = 64
RET_WIDTH = 256
ROPE_BASE = 10000.0
REL_BUCKETS = 32
REL_MAX_DIST = 128
D_FF = 2816
IN_SPLITS = (256, 512, 64, 64, 256, 64, 4, 256, 256, 256, 256)
RMS_EPS = 1e-6
GN_EPS = 1e-6

COL_DQ = 0
COL_IQ = 512
COL_RQ = 768
COL_RK = 1024
COL_RV = 1280
COL_RG = 1536
COL_KG = 1792
COL_VG = 1920
COL_IX = 2048
MAIN_W = 2304
PROJ_W = MAIN_W + S5_WIDTH
PROJ_CHUNK = 256

LANES = 128
Q_BLK = 128
K_BLK = 256
NEG_BIG = -1e30
INT_MIN = -2147483648
KEY_NEG_INF = -2139095041
FF_CHUNK = 256
N_FF_CHUNK = D_FF // FF_CHUNK
RET_CHUNK = 256
S5_TL = 64
TM_PROJ = 512
TM_FFN = 512
VMEM_LIMIT = 56 * 1024 * 1024


def _nt_dot(a, b):
    return lax.dot_general(a, b, (((1,), (1,)), ((), ())), preferred_element_type=F32)


def _dot(a, b):
    return jnp.dot(a, b, preferred_element_type=F32)


def _inproj_kernel(x_ref, g_ref, w_ref, main_ref, idx_ref, u_ref):
    x = x_ref[0]
    ms = jnp.mean(x * x, axis=-1, keepdims=True)
    h = (x * lax.rsqrt(ms + RMS_EPS) * g_ref[...]).astype(BF16)
    n_main = MAIN_W // PROJ_CHUNK
    for c in range(n_main + 1):
        acc = _dot(h, w_ref[:, c * PROJ_CHUNK:(c + 1) * PROJ_CHUNK])
        if c == n_main:
            u_ref[...] = acc
            continue
        if c * PROJ_CHUNK <= COL_VG < (c + 1) * PROJ_CHUNK:
            lane = lax.broadcasted_iota(I32, acc.shape, 1)
            acc = jnp.where(lane == COL_VG - c * PROJ_CHUNK + DSA_KV_DIM, 1.0, acc)
        if c * PROJ_CHUNK == COL_IX:
            idx_ref[0] = acc[:, :LANES]
        main_ref[0, :, c * PROJ_CHUNK:(c + 1) * PROJ_CHUNK] = acc.astype(BF16)


def _inproj(x, g, w_all):
    bsz, seq, _ = x.shape
    tm = min(TM_PROJ, seq)
    return pl.pallas_call(
        _inproj_kernel,
        grid=(bsz, seq // tm),
        in_specs=[
            pl.BlockSpec((1, tm, D_MODEL), lambda b, i: (b, i, 0)),
            pl.BlockSpec((1, D_MODEL), lambda b, i: (0, 0)),
            pl.BlockSpec((D_MODEL, PROJ_W), lambda b, i: (0, 0)),
        ],
        out_specs=[
            pl.BlockSpec((1, tm, MAIN_W), lambda b, i: (b, i, 0)),
            pl.BlockSpec((1, tm, LANES), lambda b, i: (b, i, 0)),
            pl.BlockSpec((tm, S5_WIDTH), lambda b, i: (i, b)),
        ],
        out_shape=[
            jax.ShapeDtypeStruct((bsz, seq, MAIN_W), BF16),
            jax.ShapeDtypeStruct((bsz, seq, LANES), F32),
            jax.ShapeDtypeStruct((seq, bsz * S5_WIDTH), F32),
        ],
        compiler_params=pltpu.CompilerParams(
            dimension_semantics=("arbitrary", "arbitrary"), vmem_limit_bytes=VMEM_LIMIT),
        name="inproj",
    )(x, g, w_all)


def _s5_kernel(u_ref, bmat_ref, cmat_ref, ar_ref, ai_ref, d_ref, gw_ref, gb_ref, o_ref,
               xs_ref, h_ref):
    tl, bsz, _ = u_ref.shape
    nst = S5_GROUPS * S5_STATE

    @pl.when(pl.program_id(0) == 0)
    def _():
        h_ref[...] = jnp.zeros_like(h_ref)

    u = u_ref[...].reshape(tl * bsz, S5_WIDTH)
    xs_ref[...] = _dot(u.astype(BF16), bmat_ref[...])
    ar = jnp.broadcast_to(ar_ref[...], (bsz, nst))
    ai = jnp.broadcast_to(ai_ref[...], (bsz, nst))

    def step(t, carry):
        hr, hi = carry
        rows = pl.ds(pl.multiple_of(t * bsz, bsz), bsz)
        nr = ar * hr - ai * hi + xs_ref[rows, 0:nst]
        ni = ar * hi + ai * hr + xs_ref[rows, nst:2 * nst]
        xs_ref[rows, 0:nst] = nr
        xs_ref[rows, nst:2 * nst] = ni
        return nr, ni

    hr, hi = lax.fori_loop(0, tl, step, (h_ref[:, 0:nst], h_ref[:, nst:2 * nst]), unroll=4)
    h_ref[:, 0:nst] = hr
    h_ref[:, nst:2 * nst] = hi

    y = _dot(xs_ref[...].astype(BF16), cmat_ref[...]) + d_ref[...] * u
    y = jax.nn.gelu(y)
    z = _dot(y.astype(BF16), gw_ref[...]) + gb_ref[...]
    o_ref[...] = (y * jax.nn.sigmoid(z)).reshape(tl, bsz, S5_WIDTH)


def _s5(u_t, bmat, cmat, ar, ai, dvec, gw, gb):
    seq, bsz, _ = u_t.shape
    tl = min(S5_TL, seq)
    nst2 = 2 * S5_GROUPS * S5_STATE
    const = lambda shape: pl.BlockSpec(shape, lambda i: (0,) * len(shape))
    return pl.pallas_call(
        _s5_kernel,
        grid=(seq // tl,),
        in_specs=[
            pl.BlockSpec((tl, bsz, S5_WIDTH), lambda i: (i, 0, 0)),
            const((S5_WIDTH, nst2)), const((nst2, S5_WIDTH)),
            const((1, nst2 // 2)), const((1, nst2 // 2)), const((1, S5_WIDTH)),
            const((S5_WIDTH, S5_WIDTH)), const((1, S5_WIDTH)),
        ],
        out_specs=pl.BlockSpec((tl, bsz, S5_WIDTH), lambda i: (i, 0, 0)),
        out_shape=jax.ShapeDtypeStruct((seq, bsz, S5_WIDTH), F32),
        scratch_shapes=[pltpu.VMEM((tl * bsz, nst2), F32), pltpu.VMEM((bsz, nst2), F32)],
        compiler_params=pltpu.CompilerParams(
            dimension_semantics=("arbitrary",), vmem_limit_bytes=VMEM_LIMIT),
        name="s5_scan",
    )(u_t, bmat, cmat, ar, ai, dvec, gw, gb)


def _bias_kernel(rb_ref, bk_ref, o_ref):
    h = pl.program_id(1)
    bk = bk_ref[0]
    acc = jnp.zeros(bk.shape, F32)
    for b in range(REL_BUCKETS):
        acc = jnp.where(bk == b, rb_ref[b, h], acc)
    o_ref[0] = acc


def _bucket_tiles():
    d = np.arange(4, dtype=np.int64)[:, None, None] * Q_BLK
    i = np.arange(Q_BLK, dtype=np.int64)[None, :, None]
    j = np.arange(K_BLK, dtype=np.int64)[None, None, :]
    n = np.maximum(d + i - j, 0)
    max_exact = REL_BUCKETS // 2
    nf = np.maximum(n, 1).astype(np.float32)
    large = max_exact + (np.log(nf / np.float32(max_exact))
                         / np.float32(math.log(REL_MAX_DIST / max_exact))
                         * np.float32(REL_BUCKETS - max_exact)).astype(np.int32)
    large = np.minimum(large, REL_BUCKETS - 1)
    return np.where(n < max_exact, n, large).astype(np.int32)


def _bias_tiles(rel_bias):
    buckets = jnp.asarray(_bucket_tiles())
    return pl.pallas_call(
        _bias_kernel,
        grid=(4, DSA_HEADS),
        in_specs=[
            pl.BlockSpec(memory_space=pltpu.SMEM),
            pl.BlockSpec((1, Q_BLK, K_BLK), lambda d, h: (d, 0, 0)),
        ],
        out_specs=pl.BlockSpec((1, Q_BLK, K_BLK), lambda d, h: (d, h, 0)),
        out_shape=jax.ShapeDtypeStruct((4, DSA_HEADS * Q_BLK, K_BLK), F32),
        name="rel_bias_tiles",
    )(rel_bias, buckets)


def _dsa_kernel(dq_ref, iq_ref, iw_ref, ix_ref, kg_ref, vg_ref, bias_ref, tri_ref, y_ref,
                keys_s, qs_s, iqs_s, t_s, r_s, m_s, acc_s, er_s, *, k_sel):
    qb = pl.program_id(1)
    nkb = qb // 2 + 1
    row0 = qb * Q_BLK
    sub = K_BLK // 8

    for h in range(DSA_HEADS):
        qs_s[h * Q_BLK:(h + 1) * Q_BLK, :] = (
            dq_ref[0, :, h * DSA_HEAD_DIM:(h + 1) * DSA_HEAD_DIM] * jnp.asarray(0.125, BF16))
    for h in range(IDX_HEADS):
        iqs_s[h * Q_BLK:(h + 1) * Q_BLK, :] = iq_ref[0, :, h * IDX_DIM:(h + 1) * IDX_DIM]
    w_t = iw_ref[0].T
    w_row = jnp.concatenate(
        [w_t[IDX_DIM + h:IDX_DIM + h + 1, :] for h in range(IDX_HEADS)], axis=1) * 0.0625

    key_pos = lax.broadcasted_iota(I32, (K_BLK, Q_BLK), 0)
    qry_pos = row0 + lax.broadcasted_iota(I32, (K_BLK, Q_BLK), 1)

    def score_block(kb, carry):
        ks = pl.multiple_of(kb * K_BLK, K_BLK)
        ik = ix_ref[0, pl.ds(ks, K_BLK), 0:IDX_DIM]
        r = jnp.maximum(_nt_dot(ik, iqs_s[...]), 0.0) * w_row
        s = ((r[:, 0:Q_BLK] + r[:, Q_BLK:2 * Q_BLK])
             + (r[:, 2 * Q_BLK:3 * Q_BLK] + r[:, 3 * Q_BLK:4 * Q_BLK]))
        s = jnp.where(s == 0.0, 0.0, s)
        s = jnp.where(key_pos + ks <= qry_pos, s, -jnp.inf)
        bits = lax.bitcast_convert_type(s, I32)
        keys_s[pl.ds(ks, K_BLK), :] = bits ^ ((bits >> 31) & 0x7FFFFFFF)
        return carry

    lax.fori_loop(0, nkb, score_block, 0)

    t_s[...] = jnp.full(t_s.shape, KEY_NEG_INF, I32)
    r_s[...] = jnp.zeros_like(r_s)

    @pl.when(row0 >= k_sel)
    def _():
        def count(cand, strict):
            def body(kb, acc):
                ks = pl.multiple_of(kb * K_BLK, K_BLK)
                blk = keys_s[pl.ds(ks, K_BLK), :].reshape(sub // 4, 4, 8, Q_BLK)
                hit = (blk > cand) if strict else (blk >= cand)
                return acc + jnp.sum(jnp.where(hit, 1.0, 0.0), axis=0)
            acc = lax.fori_loop(0, nkb, body, jnp.zeros((4, 8, Q_BLK), F32))
            tot = jnp.sum(jnp.sum(acc, axis=0), axis=0, keepdims=True)
            return jnp.broadcast_to(tot, (8, Q_BLK))

        def bit_step(i, tu):
            cand_u = tu | lax.shift_left(jnp.int32(1), 31 - i)
            cnt = count(cand_u ^ INT_MIN, False)
            return jnp.where(cnt >= k_sel, cand_u, tu)

        tk = lax.fori_loop(0, 32, bit_step, jnp.zeros((8, Q_BLK), I32)) ^ INT_MIN
        t_s[...] = tk
        r_s[...] = k_sel - count(tk, True)

    m_s[...] = jnp.full(m_s.shape, NEG_BIG, F32)
    acc_s[...] = jnp.zeros_like(acc_s)
    er_s[...] = jnp.zeros_like(er_s)
    t8 = t_s[...]
    r8 = r_s[...]

    def attn_block(kb, carry):
        ks = pl.multiple_of(kb * K_BLK, K_BLK)
        key = keys_s[pl.ds(ks, K_BLK), :].reshape(sub, 8, Q_BLK)
        eq = key == t8
        pre = _dot(tri_ref[...], jnp.where(eq, 1.0, 0.0).reshape(K_BLK, Q_BLK).astype(BF16))
        er = er_s[...]
        take = pre.reshape(sub, 8, Q_BLK) + er <= r8
        pen_t = jnp.where(key > t8, 0.0, jnp.where(eq, jnp.where(take, 0.0, NEG_BIG), NEG_BIG))
        er_s[...] = er + jnp.broadcast_to(pre[K_BLK - 1:K_BLK, :], (8, Q_BLK))
        pen = pen_t.reshape(K_BLK, Q_BLK).T

        s = _nt_dot(qs_s[...], kg_ref[0, pl.ds(ks, K_BLK), 0:DSA_KV_DIM])
        s = s + bias_ref[jnp.minimum(qb - 2 * kb, 3)] + jnp.concatenate([pen] * DSA_HEADS, axis=0)
        m_old = m_s[...]
        m_new = jnp.maximum(m_old, jnp.max(s, axis=1, keepdims=True))
        p = jnp.exp(s - jnp.concatenate([m_new, m_new], axis=1))
        pv = _dot(p.astype(BF16), vg_ref[0, pl.ds(ks, K_BLK), :])
        acc_s[...] = jnp.exp(m_old - m_new) * acc_s[...] + pv
        m_s[...] = m_new
        return carry

    lax.fori_loop(0, nkb, attn_block, 0)

    acc = acc_s[...]
    o = acc[:, 0:DSA_HEAD_DIM] / acc[:, DSA_HEAD_DIM:DSA_HEAD_DIM + 1]
    for h in range(DSA_HEADS):
        y_ref[0, :, h * DSA_HEAD_DIM:(h + 1) * DSA_HEAD_DIM] = (
            o[h * Q_BLK:(h + 1) * Q_BLK].astype(BF16))


def _dsa(main, idxf, bias, tri):
    bsz, seq, _ = main.shape
    k_sel = min(DSA_TOPK_MAX, seq // 4)
    assert seq % K_BLK == 0 and k_sel % Q_BLK == 0
    col = lambda c, w: c // w
    return pl.pallas_call(
        functools.partial(_dsa_kernel, k_sel=k_sel),
        grid=(bsz, seq // Q_BLK),
        in_specs=[
            pl.BlockSpec((1, Q_BLK, DSA_WIDTH), lambda b, q: (b, q, col(COL_DQ, DSA_WIDTH))),
            pl.BlockSpec((1, Q_BLK, 256), lambda b, q: (b, q, col(COL_IQ, 256))),
            pl.BlockSpec((1, Q_BLK, LANES), lambda b, q: (b, q, 0)),
            pl.BlockSpec((1, seq, LANES), lambda b, q: (b, 0, col(COL_IX, LANES))),
            pl.BlockSpec((1, seq, LANES), lambda b, q: (b, 0, col(COL_KG, LANES))),
            pl.BlockSpec((1, seq, LANES), lambda b, q: (b, 0, col(COL_VG, LANES))),
            pl.BlockSpec((4, DSA_HEADS * Q_BLK, K_BLK), lambda b, q: (0, 0, 0)),
            pl.BlockSpec((K_BLK, K_BLK), lambda b, q: (0, 0)),
        ],
        out_specs=pl.BlockSpec((1, Q_BLK, DSA_WIDTH), lambda b, q: (b, q, 0)),
        out_shape=jax.ShapeDtypeStruct((bsz, seq, DSA_WIDTH), BF16),
        scratch_shapes=[
            pltpu.VMEM((seq, Q_BLK), I32),
            pltpu.VMEM((DSA_HEADS * Q_BLK, DSA_HEAD_DIM), BF16),
            pltpu.VMEM((IDX_HEADS * Q_BLK, IDX_DIM), BF16),
            pltpu.VMEM((8, Q_BLK), I32),
            pltpu.VMEM((8, Q_BLK), F32),
            pltpu.VMEM((DSA_HEADS * Q_BLK, LANES), F32),
            pltpu.VMEM((DSA_HEADS * Q_BLK, LANES), F32),
            pltpu.VMEM((8, Q_BLK), F32),
        ],
        compiler_params=pltpu.CompilerParams(
            dimension_semantics=("arbitrary", "arbitrary"), vmem_limit_bytes=VMEM_LIMIT),
        name="dsa_attention",
    )(main, main, idxf, main, main, main, bias, tri)


def _ret_kernel(q_ref, k_ref, v_ref, g_ref, cos_ref, sa_ref, sb_ref, dec_ref, wq_ref, wk_ref,
                gc_ref, gn_ref, o_ref, st_ref):
    @pl.when(pl.program_id(1) == 0)
    def _():
        st_ref[...] = jnp.zeros_like(st_ref)

    cos, sa, sb = cos_ref[...], sa_ref[...], sb_ref[...]
    half = RET_HEAD_DIM // 2

    def rot(x):
        return (x * cos + pltpu.roll(x, RET_WIDTH - half, 1) * sa + pltpu.roll(x, half, 1) * sb)

    q = rot(q_ref[0].astype(F32))
    k = rot(k_ref[0].astype(F32)) * (RET_HEAD_DIM ** -0.5)
    v = v_ref[0]
    g = g_ref[0].astype(F32)
    gate = g * jax.nn.sigmoid(g)
    gn = gn_ref[...]
    for h in range(RET_HEADS):
        sl = slice(h * RET_HEAD_DIM, (h + 1) * RET_HEAD_DIM)
        qh = q[:, sl].astype(BF16)
        kf = k[:, sl]
        vh = v[:, sl]
        s = _nt_dot(qh, kf.astype(BF16)) * dec_ref[h]
        st = st_ref[h]
        o = _dot(s.astype(BF16), vh) + _dot(qh, st.astype(BF16)) * wq_ref[h]
        kw_t = (kf * wk_ref[h]).T.astype(BF16)
        st_ref[h] = gc_ref[h] * st + _dot(kw_t, vh)
        mu = jnp.mean(o, axis=-1, keepdims=True)
        d = o - mu
        var = jnp.mean(d * d, axis=-1, keepdims=True)
        on = d * lax.rsqrt(var + GN_EPS)
        o_ref[0, :, sl] = (gate[:, sl] * (on * gn[:, sl])).astype(BF16)


def _ret_tables(seq, chunk):
    half = RET_HEAD_DIM // 2
    inv = ROPE_BASE ** (-jnp.arange(half, dtype=F32) / half)
    ang = jnp.arange(seq, dtype=jnp.int32).astype(F32)[:, None] * inv[None, :]
    cos, sin = jnp.cos(ang), jnp.sin(ang)
    zero = jnp.zeros_like(sin)
    cos_t = jnp.tile(cos, (1, 2 * RET_HEADS))
    sa = jnp.tile(jnp.concatenate([-sin, zero], axis=1), (1, RET_HEADS))
    sb = jnp.tile(jnp.concatenate([zero, sin], axis=1), (1, RET_HEADS))
    gamma = 1.0 - 2.0 ** (-5.0 - jnp.arange(RET_HEADS, dtype=F32))
    log_g = jnp.log(gamma)
    i = jnp.arange(chunk, dtype=F32)
    diff = i[:, None] - i[None, :]
    decay = jnp.where(diff >= 0, jnp.exp(log_g[:, None, None] * jnp.maximum(diff, 0.0)), 0.0)
    ones = jnp.ones((1, 1, RET_HEAD_DIM), F32)
    wq = jnp.exp(log_g[:, None] * (i + 1.0)[None, :])[:, :, None] * ones
    wk = jnp.exp(log_g[:, None] * (chunk - 1.0 - i)[None, :])[:, :, None] * ones
    gc = jnp.exp(log_g * chunk)[:, None, None] * jnp.ones((1, RET_HEAD_DIM, RET_HEAD_DIM), F32)
    return cos_t, sa, sb, decay, wq, wk, gc


def _retention(main, tables, gn):
    bsz, seq, _ = main.shape
    cos_t, sa, sb, decay, wq, wk, gc = tables
    c = decay.shape[-1]
    blk = lambda col: pl.BlockSpec((1, c, RET_WIDTH), lambda b, i: (b, i, col // RET_WIDTH))
    tab = pl.BlockSpec((c, RET_WIDTH), lambda b, i: (i, 0))
    const = lambda shape: pl.BlockSpec(shape, lambda b, i: (0,) * len(shape))
    return pl.pallas_call(
        _ret_kernel,
        grid=(bsz, seq // c),
        in_specs=[
            blk(COL_RQ), blk(COL_RK), blk(COL_RV), blk(COL_RG), tab, tab, tab,
            const((RET_HEADS, c, c)), const((RET_HEADS, c, RET_HEAD_DIM)),
            const((RET_HEADS, c, RET_HEAD_DIM)),
            const((RET_HEADS, RET_HEAD_DIM, RET_HEAD_DIM)), const((1, RET_WIDTH)),
        ],
        out_specs=pl.BlockSpec((1, c, RET_WIDTH), lambda b, i: (b, i, 0)),
        out_shape=jax.ShapeDtypeStruct((bsz, seq, RET_WIDTH), BF16),
        scratch_shapes=[pltpu.VMEM((RET_HEADS, RET_HEAD_DIM, RET_HEAD_DIM), F32)],
        compiler_params=pltpu.CompilerParams(
            dimension_semantics=("arbitrary", "arbitrary"), vmem_limit_bytes=VMEM_LIMIT),
        name="retention",
    )(main, main, main, main, cos_t, sa, sb, decay, wq, wk, gc, gn)


def _ffn_kernel(x_ref, ya_ref, yb_ref, yc_ref, wo_ref, g_ref, wg_ref, wu_ref, wd_ref, gf_ref,
                o_ref, x1_s, h_s, acc_s, *, final):
    x1 = (x_ref[0]
          + _dot(ya_ref[...].astype(BF16), wo_ref[0:S5_WIDTH, :])
          + _dot(yb_ref[0], wo_ref[S5_WIDTH:S5_WIDTH + DSA_WIDTH, :])
          + _dot(yc_ref[0], wo_ref[S5_WIDTH + DSA_WIDTH:, :]))
    x1_s[...] = x1
    ms = jnp.mean(x1 * x1, axis=-1, keepdims=True)
    h_s[...] = (x1 * lax.rsqrt(ms + RMS_EPS) * g_ref[...]).astype(BF16)
    acc_s[...] = jnp.zeros_like(acc_s)

    def chunk(c, carry):
        h = h_s[...]
        gate = _dot(h, wg_ref[c])
        up = _dot(h, wu_ref[c])
        a = (gate * jax.nn.sigmoid(gate) * up).astype(BF16)
        acc_s[...] += _dot(a, wd_ref[c])
        return carry

    lax.fori_loop(0, N_FF_CHUNK, chunk, 0)
    y = x1_s[...] + acc_s[...]
    if final:
        ms = jnp.mean(y * y, axis=-1, keepdims=True)
        y = y * lax.rsqrt(ms + RMS_EPS) * gf_ref[...]
    o_ref[0] = y


def _ffn(x, ya, yb, yc, wo, g, wg, wu, wd, gf, final):
    bsz, seq, _ = x.shape
    tm = min(TM_FFN, seq)
    once = pl.Buffered(1)
    const = lambda shape: pl.BlockSpec(shape, lambda b, i: (0,) * len(shape), pipeline_mode=once)
    return pl.pallas_call(
        functools.partial(_ffn_kernel, final=final),
        grid=(bsz, seq // tm),
        in_specs=[
            pl.BlockSpec((1, tm, D_MODEL), lambda b, i: (b, i, 0)),
            pl.BlockSpec((tm, S5_WIDTH), lambda b, i: (i, b)),
            pl.BlockSpec((1, tm, DSA_WIDTH), lambda b, i: (b, i, 0)),
            pl.BlockSpec((1, tm, RET_WIDTH), lambda b, i: (b, i, 0)),
            const((D_MODEL, D_MODEL)), const((1, D_MODEL)),
            const((N_FF_CHUNK, D_MODEL, FF_CHUNK)), const((N_FF_CHUNK, D_MODEL, FF_CHUNK)),
            const((N_FF_CHUNK, FF_CHUNK, D_MODEL)), const((1, D_MODEL)),
        ],
        out_specs=pl.BlockSpec((1, tm, D_MODEL), lambda b, i: (b, i, 0)),
        out_shape=jax.ShapeDtypeStruct((bsz, seq, D_MODEL), F32),
        scratch_shapes=[pltpu.VMEM((tm, D_MODEL), F32), pltpu.VMEM((tm, D_MODEL), BF16),
                        pltpu.VMEM((tm, D_MODEL), F32)],
        compiler_params=pltpu.CompilerParams(
            dimension_semantics=("arbitrary", "arbitrary"), vmem_limit_bytes=VMEM_LIMIT),
        name="outproj_ffn",
    )(x, ya, yb, yc, wo, g, wg, wu, wd, gf)


def _pack_w_in(w_in):
    pts = [int(s) for s in np.cumsum(IN_SPLITS)[:-1]]
    u, dq, dk, dv, iq, ik, iw, rq, rk, rv, rg = jnp.split(w_in, pts, axis=-1)
    z = lambda n: jnp.zeros(w_in.shape[:-1] + (n,), w_in.dtype)
    parts = [dq, iq, rq, rk, rv, rg, dk, z(64), dv, z(64), ik, iw, z(60), z(128), u]
    return jnp.concatenate(parts, axis=-1).astype(BF16)


def _s5_params(a_re, a_im, log_dt, b_re, b_im, c_re, c_im, d_skip):
    g, n, p = S5_GROUPS, S5_STATE, S5_GROUP
    dt = jnp.exp(log_dt)[:, None]
    mag = jnp.exp(a_re * dt)
    ang = a_im * dt
    abr, abi = mag * jnp.cos(ang), mag * jnp.sin(ang)
    den = a_re * a_re + a_im * a_im
    nr, ni = abr - 1.0, abi
    cr = (nr * a_re + ni * a_im) / den
    ci = (ni * a_re - nr * a_im) / den
    bbr = cr[..., None] * b_re - ci[..., None] * b_im
    bbi = cr[..., None] * b_im + ci[..., None] * b_re
    eye = jnp.eye(g, dtype=F32)

    def b_block(bb):
        return (jnp.swapaxes(bb, 1, 2)[:, :, None, :] * eye[:, None, :, None]).reshape(g * p, g * n)

    def c_block(cc):
        return (jnp.swapaxes(cc, 1, 2)[:, :, None, :] * eye[:, None, :, None]).reshape(g * n, g * p)

    bmat = jnp.concatenate([b_block(bbr), b_block(bbi)], axis=1).astype(BF16)
    cmat = jnp.concatenate([c_block(c_re), -c_block(c_im)], axis=0).astype(BF16)
    return bmat, cmat, abr.reshape(1, g * n), abi.reshape(1, g * n), d_skip.reshape(1, g * p)


def _tri_lower():
    j = np.arange(K_BLK)
    return jnp.asarray((j[None, :] <= j[:, None]).astype(np.float32), BF16)


def kernel(x, w_in, w_out, norm_mix, ssm_a_re, ssm_a_im, ssm_log_dt, ssm_b_re, ssm_b_im,
           ssm_c_re, ssm_c_im, ssm_d, ssm_glu_w, ssm_glu_b, rel_bias, ret_gn, norm_ffn,
           w_ffn_in, w_ffn_out, norm_final):
    bsz, seq, _ = x.shape
    depth = w_in.shape[0]
    w_all = _pack_w_in(w_in)
    wo = w_out.astype(BF16)
    wg = w_ffn_in[:, :, :D_FF].reshape(depth, D_MODEL, N_FF_CHUNK, FF_CHUNK)
    wu = w_ffn_in[:, :, D_FF:].reshape(depth, D_MODEL, N_FF_CHUNK, FF_CHUNK)
    wg = jnp.swapaxes(wg, 1, 2).astype(BF16)
    wu = jnp.swapaxes(wu, 1, 2).astype(BF16)
    wd = w_ffn_out.reshape(depth, N_FF_CHUNK, FF_CHUNK, D_MODEL).astype(BF16)
    glu_w = ssm_glu_w.astype(BF16)

    bias = _bias_tiles(rel_bias)
    tri = _tri_lower()
    tables = _ret_tables(seq, min(RET_CHUNK, seq))

    for l in range(depth):
        main, idxf, u_t = _inproj(x, norm_mix[l][None, :], w_all[l])
        s5p = _s5_params(ssm_a_re[l], ssm_a_im[l], ssm_log_dt[l], ssm_b_re[l], ssm_b_im[l],
                         ssm_c_re[l], ssm_c_im[l], ssm_d[l])
        ya = _s5(u_t.reshape(seq, bsz, S5_WIDTH), *s5p, glu_w[l], ssm_glu_b[l][None, :])
        yb = _dsa(main, idxf, bias, tri)
        yc = _retention(main, tables, ret_gn[l][None, :])
        x = _ffn(x, ya.reshape(seq, bsz * S5_WIDTH), yb, yc, wo[l], norm_ffn[l][None, :],
                 wg[l], wu[l], wd[l], norm_final[None, :], final=(l == depth - 1))
    return x
```

```python
import functools
import math

import numpy as np
import jax
import jax.numpy as jnp
from jax import lax
from jax.experimental import pallas as pl
from jax.experimental.pallas import tpu as pltpu

F32 = jnp.float32
BF16 = jnp.bfloat16
I32 = jnp.int32
I16 = jnp.int16

D_MODEL = 1024
DEPTH = 4
S5_WIDTH = 256
S5_GROUP = 16
S5_GROUPS = 16
S5_STATE = 64
DSA_HEADS = 8
DSA_HEAD_DIM = 64
DSA_WIDTH = 512
DSA_KV_DIM = 64
IDX_HEADS = 4
IDX_DIM = 64
DSA_TOPK_MAX = 256
RET_HEADS = 4
RET_HEAD_DIM = 64
RET_WIDTH = 256
ROPE_BASE = 10000.0
REL_BUCKETS = 32
REL_MAX_DIST = 128
D_FF = 2816
IN_SPLITS = (256, 512, 64, 64, 256, 64, 4, 256, 256, 256, 256)
RMS_EPS = 1e-6
GN_EPS = 1e-6

COL_DQ = 0
COL_IQ = 512
COL_RQ = 768
COL_RK = 1024
COL_RV = 1280
COL_RG = 1536
COL_KG = 1792
COL_VG = 1920
COL_IX = 2048
MAIN_W = 2304
ONES_COLS = (COL_KG + 64, COL_KG + 65, COL_VG + 64)
PROJ_W = MAIN_W + S5_WIDTH
PROJ_CHUNK = 256

LANES = 128
Q_BLK = 128
Q_PAIR = 2
K_BLK = 256
NEG_BIG = -1e30
INT_MIN = -2147483648
HALF16 = 32768
KEY_NEG_INF = -2139095041
FF_CHUNK = 256
N_FF_CHUNK = D_FF // FF_CHUNK
RET_CHUNK = 256
S5_TL = 64
TM_PROJ = 512
TM_FFN = 512
VMEM_LIMIT = 56 * 1024 * 1024


def _nt_dot(a, b):
    return lax.dot_general(a, b, (((1,), (1,)), ((), ())), preferred_element_type=F32)


def _dot(a, b):
    return jnp.dot(a, b, preferred_element_type=F32)


def _inproj_kernel(x_ref, g_ref, w_ref, main_ref, idx_ref, u_ref):
    x = x_ref[0]
    ms = jnp.mean(x * x, axis=-1, keepdims=True)
    h = (x * lax.rsqrt(ms + RMS_EPS) * g_ref[...]).astype(BF16)
    n_main = MAIN_W // PROJ_CHUNK
    for c in range(n_main + 1):
        acc = _dot(h, w_ref[:, c * PROJ_CHUNK:(c + 1) * PROJ_CHUNK])
        if c == n_main:
            u_ref[...] = acc
            continue
        for one_col in ONES_COLS:
            if c * PROJ_CHUNK <= one_col < (c + 1) * PROJ_CHUNK:
                lane = lax.broadcasted_iota(I32, acc.shape, 1)
                acc = jnp.where(lane == one_col - c * PROJ_CHUNK, 1.0, acc)
        if c * PROJ_CHUNK == COL_IX:
            idx_ref[0] = acc[:, :LANES]
        main_ref[0, :, c * PROJ_CHUNK:(c + 1) * PROJ_CHUNK] = acc.astype(BF16)


def _inproj(x, g, w_all):
    bsz, seq, _ = x.shape
    tm = min(TM_PROJ, seq)
    return pl.pallas_call(
        _inproj_kernel,
        grid=(bsz, seq // tm),
        in_specs=[
            pl.BlockSpec((1, tm, D_MODEL), lambda b, i: (b, i, 0)),
            pl.BlockSpec((1, D_MODEL), lambda b, i: (0, 0)),
            pl.BlockSpec((D_MODEL, PROJ_W), lambda b, i: (0, 0)),
        ],
        out_specs=[
            pl.BlockSpec((1, tm, MAIN_W), lambda b, i: (b, i, 0)),
            pl.BlockSpec((1, tm, LANES), lambda b, i: (b, i, 0)),
            pl.BlockSpec((tm, S5_WIDTH), lambda b, i: (i, b)),
        ],
        out_shape=[
            jax.ShapeDtypeStruct((bsz, seq, MAIN_W), BF16),
            jax.ShapeDtypeStruct((bsz, seq, LANES), F32),
            jax.ShapeDtypeStruct((seq, bsz * S5_WIDTH), F32),
        ],
        compiler_params=pltpu.CompilerParams(
            dimension_semantics=("arbitrary", "arbitrary"), vmem_limit_bytes=VMEM_LIMIT),
        name="inproj",
    )(x, g, w_all)


def _s5_kernel(u_ref, bmat_ref, cmat_ref, ar_ref, ai_ref, d_ref, gw_ref, gb_ref, o_ref,
               xs_ref, h_ref):
    tl, bsz, _ = u_ref.shape
    nst = S5_GROUPS * S5_STATE

    @pl.when(pl.program_id(0) == 0)
    def _():
        h_ref[...] = jnp.zeros_like(h_ref)

    u = u_ref[...].reshape(tl * bsz, S5_WIDTH)
    xs_ref[...] = _dot(u.astype(BF16), bmat_ref[...])
    ar = jnp.broadcast_to(ar_ref[...], (bsz, nst))
    ai = jnp.broadcast_to(ai_ref[...], (bsz, nst))

    def step(t, carry):
        hr, hi = carry
        rows = pl.ds(pl.multiple_of(t * bsz, bsz), bsz)
        nr = ar * hr - ai * hi + xs_ref[rows, 0:nst]
        ni = ar * hi + ai * hr + xs_ref[rows, nst:2 * nst]
        xs_ref[rows, 0:nst] = nr
        xs_ref[rows, nst:2 * nst] = ni
        return nr, ni

    hr, hi = lax.fori_loop(0, tl, step, (h_ref[:, 0:nst], h_ref[:, nst:2 * nst]), unroll=4)
    h_ref[:, 0:nst] = hr
    h_ref[:, nst:2 * nst] = hi

    y = _dot(xs_ref[...].astype(BF16), cmat_ref[...]) + d_ref[...] * u
    y = jax.nn.gelu(y)
    z = _dot(y.astype(BF16), gw_ref[...]) + gb_ref[...]
    o_ref[...] = (y * jax.nn.sigmoid(z)).reshape(tl, bsz, S5_WIDTH)


def _s5(u_t, bmat, cmat, ar, ai, dvec, gw, gb):
    seq, bsz, _ = u_t.shape
    tl = min(S5_TL, seq)
    nst2 = 2 * S5_GROUPS * S5_STATE
    const = lambda shape: pl.BlockSpec(shape, lambda i: (0,) * len(shape))
    return pl.pallas_call(
        _s5_kernel,
        grid=(seq // tl,),
        in_specs=[
            pl.BlockSpec((tl, bsz, S5_WIDTH), lambda i: (i, 0, 0)),
            const((S5_WIDTH, nst2)), const((nst2, S5_WIDTH)),
            const((1, nst2 // 2)), const((1, nst2 // 2)), const((1, S5_WIDTH)),
            const((S5_WIDTH, S5_WIDTH)), const((1, S5_WIDTH)),
        ],
        out_specs=pl.BlockSpec((tl, bsz, S5_WIDTH), lambda i: (i, 0, 0)),
        out_shape=jax.ShapeDtypeStruct((seq, bsz, S5_WIDTH), F32),
        scratch_shapes=[pltpu.VMEM((tl * bsz, nst2), F32), pltpu.VMEM((bsz, nst2), F32)],
        compiler_params=pltpu.CompilerParams(
            dimension_semantics=("arbitrary",), vmem_limit_bytes=VMEM_LIMIT),
        name="s5_scan",
    )(u_t, bmat, cmat, ar, ai, dvec, gw, gb)


def _bias_kernel(rb_ref, bk_ref, o_ref):
    h = pl.program_id(1)
    bk = bk_ref[0]
    acc = jnp.zeros(bk.shape, F32)
    for b in range(REL_BUCKETS):
        acc = jnp.where(bk == b, rb_ref[b, h], acc)
    o_ref[0] = acc - rb_ref[REL_BUCKETS - 1, h]


def _bucket_tiles():
    d = np.arange(4, dtype=np.int64)[:, None, None] * Q_BLK
    i = np.arange(Q_BLK, dtype=np.int64)[None, :, None]
    j = np.arange(K_BLK, dtype=np.int64)[None, None, :]
    n = np.maximum(d + i - j, 0)
    max_exact = REL_BUCKETS // 2
    nf = np.maximum(n, 1).astype(np.float32)
    large = max_exact + (np.log(nf / np.float32(max_exact))
                         / np.float32(math.log(REL_MAX_DIST / max_exact))
                         * np.float32(REL_BUCKETS - max_exact)).astype(np.int32)
    large = np.minimum(large, REL_BUCKETS - 1)
    return np.where(n < max_exact, n, large).astype(np.int32)


def _bias_tiles(rel_bias):
    buckets = jnp.asarray(_bucket_tiles())
    return pl.pallas_call(
        _bias_kernel,
        grid=(4, DSA_HEADS),
        in_specs=[
            pl.BlockSpec(memory_space=pltpu.SMEM),
            pl.BlockSpec((1, Q_BLK, K_BLK), lambda d, h: (d, 0, 0)),
        ],
        out_specs=pl.BlockSpec((1, Q_BLK, K_BLK), lambda d, h: (d, h, 0)),
        out_shape=jax.ShapeDtypeStruct((4, DSA_HEADS * Q_BLK, K_BLK), F32),
        name="rel_bias_tiles",
    )(rel_bias, buckets)


def _dsa_kernel(dq_ref, iq_ref, iw_ref, ix_ref, kg_ref, vg_ref, bias_ref, tri_ref, eye_ref,
                cfar_ref, y_ref, keys_s, kh_s, kl_s, qs_s, iqs_s, t_s, r_s, m_s, acc_s, er_s,
                *, k_sel):
    qp = pl.program_id(1)
    nkb = qp + 1
    sub = K_BLK // 8
    rows = lambda p: slice(p * Q_BLK, (p + 1) * Q_BLK)
    hrows = lambda h: slice(h * Q_BLK, (h + 1) * Q_BLK)

    w_rows = []
    for p in range(Q_PAIR):
        qs_s[p, :, 0:Q_BLK] = eye_ref[...]
        for h in range(DSA_HEADS):
            dq = dq_ref[0, rows(p), h * DSA_HEAD_DIM:(h + 1) * DSA_HEAD_DIM]
            qs_s[p, hrows(h), Q_BLK:] = jnp.concatenate(
                [dq * jnp.asarray(0.125, BF16), cfar_ref[h]], axis=1)
        for h in range(IDX_HEADS):
            iqs_s[p * IDX_HEADS * Q_BLK + h * Q_BLK:p * IDX_HEADS * Q_BLK + (h + 1) * Q_BLK, :] = (
                iq_ref[0, rows(p), h * IDX_DIM:(h + 1) * IDX_DIM])
        w_t = iw_ref[0, rows(p), :].T
        w_rows.append(jnp.concatenate(
            [w_t[IDX_DIM + h:IDX_DIM + h + 1, :] for h in range(IDX_HEADS)], axis=1) * 0.0625)
    w_row = jnp.concatenate(w_rows, axis=1)

    key_pos = lax.broadcasted_iota(I32, (K_BLK, Q_BLK), 0)
    lane_pos = lax.broadcasted_iota(I32, (K_BLK, Q_BLK), 1)

    def score_block(kb, carry):
        ks = pl.multiple_of(kb * K_BLK, K_BLK)
        ik = ix_ref[0, pl.ds(ks, K_BLK), 0:IDX_DIM]
        r = jnp.maximum(_nt_dot(ik, iqs_s[...]), 0.0) * w_row
        for p in range(Q_PAIR):
            c0 = p * IDX_HEADS * Q_BLK
            s = ((r[:, c0:c0 + Q_BLK] + r[:, c0 + Q_BLK:c0 + 2 * Q_BLK])
                 + (r[:, c0 + 2 * Q_BLK:c0 + 3 * Q_BLK] + r[:, c0 + 3 * Q_BLK:c0 + 4 * Q_BLK]))
            s = jnp.where(s == 0.0, 0.0, s)
            qry_pos = (Q_PAIR * qp + p) * Q_BLK + lane_pos
            s = jnp.where(key_pos + ks <= qry_pos, s, -jnp.inf)
            bits = lax.bitcast_convert_type(s, I32)
            key = bits ^ ((bits >> 31) & 0x7FFFFFFF)
            keys_s[p, pl.ds(ks, K_BLK), :] = key
            kh_s[p, pl.ds(ks, K_BLK), :] = (key >> 16).astype(I16)
            kl_s[p, pl.ds(ks, K_BLK), :] = ((key & 0xFFFF) - HALF16).astype(I16)
        return carry

    lax.fori_loop(0, nkb, score_block, 0)

    t_s[...] = jnp.full(t_s.shape, KEY_NEG_INF, I32)
    r_s[...] = jnp.zeros_like(r_s)

    @pl.when(qp * Q_PAIR * Q_BLK >= k_sel)
    def _():
        sub16 = K_BLK // 16

        def count16(ref, cand):
            c16 = jnp.concatenate([cand, cand], axis=1).astype(I16)[:, None, None]
            def body(kb, acc):
                ks = pl.multiple_of(kb * K_BLK, K_BLK)
                blk = ref[:, pl.ds(ks, K_BLK), :].reshape(Q_PAIR, sub16 // 4, 4, 16, Q_BLK)
                one = jnp.ones(blk.shape, I16)
                hit = jnp.where(blk >= c16, one, jnp.zeros_like(one))
                for j in range(sub16 // 4):
                    acc = acc + hit[:, j]
                return acc
            acc = lax.fori_loop(0, nkb, body, jnp.zeros((Q_PAIR, 4, 16, Q_BLK), I16))
            tot = jnp.sum(jnp.sum(acc.astype(I32), axis=1), axis=1, keepdims=True)
            return jnp.broadcast_to(tot, (Q_PAIR, 8, Q_BLK))

        def bisect16(ref, target):
            def bit_step(i, tu):
                cand_u = tu | lax.shift_left(jnp.int32(1), 15 - i)
                return jnp.where(count16(ref, cand_u - HALF16) >= target, cand_u, tu)
            return lax.fori_loop(0, 16, bit_step, jnp.zeros((Q_PAIR, 8, Q_BLK), I32))

        hi = bisect16(kh_s, k_sel) - HALF16
        need = k_sel - count16(kh_s, hi + 1)
        hi16 = jnp.concatenate([hi, hi], axis=1).astype(I16)

        def keep_equal_high(kb, carry):
            ks = pl.multiple_of(kb * K_BLK, K_BLK)
            same = kh_s[:, pl.ds(ks, K_BLK), :].reshape(Q_PAIR, sub16, 16, Q_BLK) == hi16[:, None]
            low = kl_s[:, pl.ds(ks, K_BLK), :].reshape(Q_PAIR, sub16, 16, Q_BLK)
            kl_s[:, pl.ds(ks, K_BLK), :] = jnp.where(
                same, low, jnp.full(low.shape, -HALF16, I16)).reshape(Q_PAIR, K_BLK, Q_BLK)
            return carry

        lax.fori_loop(0, nkb, keep_equal_high, 0)
        tk = hi * (2 * HALF16) + bisect16(kl_s, need)

        def count_gt(kb, acc):
            ks = pl.multiple_of(kb * K_BLK, K_BLK)
            blk = keys_s[:, pl.ds(ks, K_BLK), :].reshape(Q_PAIR, sub // 4, 4, 8, Q_BLK)
            return acc + jnp.sum(jnp.where(blk > tk[:, None, None], 1.0, 0.0), axis=1)

        acc = lax.fori_loop(0, nkb, count_gt, jnp.zeros((Q_PAIR, 4, 8, Q_BLK), F32))
        t_s[...] = tk
        r_s[...] = k_sel - jnp.broadcast_to(
            jnp.sum(jnp.sum(acc, axis=1), axis=1, keepdims=True), (Q_PAIR, 8, Q_BLK))

    m_s[...] = jnp.full(m_s.shape, NEG_BIG, F32)
    acc_s[...] = jnp.zeros_like(acc_s)
    er_s[...] = jnp.zeros_like(er_s)

    def attn_block(kb, carry, near):
        ks = pl.multiple_of(kb * K_BLK, K_BLK)
        kgb = kg_ref[0, pl.ds(ks, K_BLK), :]
        vgb = vg_ref[0, pl.ds(ks, K_BLK), :]
        for p in range(Q_PAIR):
            t8, r8 = t_s[p], r_s[p]
            key = keys_s[p, pl.ds(ks, K_BLK), :].reshape(sub, 8, Q_BLK)
            eq = key == t8
            pre = _dot(tri_ref[...], jnp.where(eq, 1.0, 0.0).reshape(K_BLK, Q_BLK).astype(BF16))
            er = er_s[p]
            take = pre.reshape(sub, 8, Q_BLK) + er <= r8
            pen_t = jnp.where(key > t8, 0.0,
                              jnp.where(eq, jnp.where(take, 0.0, NEG_BIG), NEG_BIG))
            er_s[p] = er + jnp.broadcast_to(pre[K_BLK - 1:K_BLK, :], (8, Q_BLK))

            rhs = jnp.concatenate([pen_t.reshape(K_BLK, Q_BLK).astype(BF16), kgb], axis=1)
            s = _nt_dot(qs_s[p], rhs)
            if near:
                s = s + bias_ref[jnp.minimum(Q_PAIR * qp + p - 2 * kb, 3)]
            m_old = m_s[p]
            m_new = jnp.maximum(m_old, jnp.max(s, axis=1, keepdims=True))
            pr = jnp.exp(s - jnp.concatenate([m_new, m_new], axis=1))
            pv = _dot(pr.astype(BF16), vgb)
            acc_s[p] = jnp.exp(m_old - m_new) * acc_s[p] + pv
            m_s[p] = m_new
        return carry

    n_far = jnp.maximum(nkb - 2, 0)
    lax.fori_loop(0, n_far, functools.partial(attn_block, near=False), 0)
    lax.fori_loop(n_far, nkb, functools.partial(attn_block, near=True), 0)

    for p in range(Q_PAIR):
        acc = acc_s[p]
        o = acc[:, 0:DSA_HEAD_DIM] / acc[:, DSA_HEAD_DIM:DSA_HEAD_DIM + 1]
        for h in range(DSA_HEADS):
            y_ref[0, rows(p), h * DSA_HEAD_DIM:(h + 1) * DSA_HEAD_DIM] = o[hrows(h)].astype(BF16)


def _dsa(main, idxf, bias, tri, eye, cfar):
    bsz, seq, _ = main.shape
    k_sel = min(DSA_TOPK_MAX, seq // 4)
    qrows = Q_PAIR * Q_BLK
    assert qrows == K_BLK and seq % K_BLK == 0 and k_sel % qrows == 0
    col = lambda c, w: c // w
    return pl.pallas_call(
        functools.partial(_dsa_kernel, k_sel=k_sel),
        grid=(bsz, seq // qrows),
        in_specs=[
            pl.BlockSpec((1, qrows, DSA_WIDTH), lambda b, q: (b, q, col(COL_DQ, DSA_WIDTH))),
            pl.BlockSpec((1, qrows, 256), lambda b, q: (b, q, col(COL_IQ, 256))),
            pl.BlockSpec((1, qrows, LANES), lambda b, q: (b, q, 0)),
            pl.BlockSpec((1, seq, LANES), lambda b, q: (b, 0, col(COL_IX, LANES))),
            pl.BlockSpec((1, seq, LANES), lambda b, q: (b, 0, col(COL_KG, LANES))),
            pl.BlockSpec((1, seq, LANES), lambda b, q: (b, 0, col(COL_VG, LANES))),
            pl.BlockSpec((4, DSA_HEADS * Q_BLK, K_BLK), lambda b, q: (0, 0, 0)),
            pl.BlockSpec((K_BLK, K_BLK), lambda b, q: (0, 0)),
            pl.BlockSpec((DSA_HEADS * Q_BLK, Q_BLK), lambda b, q: (0, 0)),
            pl.BlockSpec((DSA_HEADS, Q_BLK, DSA_HEAD_DIM), lambda b, q: (0, 0, 0)),
        ],
        out_specs=pl.BlockSpec((1, qrows, DSA_WIDTH), lambda b, q: (b, q, 0)),
        out_shape=jax.ShapeDtypeStruct((bsz, seq, DSA_WIDTH), BF16),
        scratch_shapes=[
            pltpu.VMEM((Q_PAIR, seq, Q_BLK), I32),
            pltpu.VMEM((Q_PAIR, seq, Q_BLK), I16),
            pltpu.VMEM((Q_PAIR, seq, Q_BLK), I16),
            pltpu.VMEM((Q_PAIR, DSA_HEADS * Q_BLK, 2 * Q_BLK), BF16),
            pltpu.VMEM((Q_PAIR * IDX_HEADS * Q_BLK, IDX_DIM), BF16),
            pltpu.VMEM((Q_PAIR, 8, Q_BLK), I32),
            pltpu.VMEM((Q_PAIR, 8, Q_BLK), F32),
            pltpu.VMEM((Q_PAIR, DSA_HEADS * Q_BLK, LANES), F32),
            pltpu.VMEM((Q_PAIR, DSA_HEADS * Q_BLK, LANES), F32),
            pltpu.VMEM((Q_PAIR, 8, Q_BLK), F32),
        ],
        compiler_params=pltpu.CompilerParams(
            dimension_semantics=("arbitrary", "arbitrary"), vmem_limit_bytes=VMEM_LIMIT),
        name="dsa_attention",
    )(main, main, idxf, main, main, main, bias, tri, eye, cfar)


def _ret_kernel(q_ref, k_ref, v_ref, g_ref, cos_ref, sa_ref, sb_ref, dec_ref, wq_ref, wk_ref,
                gc_ref, gn_ref, o_ref, st_ref):
    @pl.when(pl.program_id(1) == 0)
    def _():
        st_ref[...] = jnp.zeros_like(st_ref)

    cos, sa, sb = cos_ref[...], sa_ref[...], sb_ref[...]
    half = RET_HEAD_DIM // 2

    def rot(x):
        return (x * cos + pltpu.roll(x, RET_WIDTH - half, 1) * sa + pltpu.roll(x, half, 1) * sb)

    q = rot(q_ref[0].astype(F32))
    k = rot(k_ref[0].astype(F32)) * (RET_HEAD_DIM ** -0.5)
    v = v_ref[0]
    g = g_ref[0].astype(F32)
    gate = g * jax.nn.sigmoid(g)
    gn = gn_ref[...]
    for h in range(RET_HEADS):
        sl = slice(h * RET_HEAD_DIM, (h + 1) * RET_HEAD_DIM)
        qh = q[:, sl].astype(BF16)
        kf = k[:, sl]
        vh = v[:, sl]
        s = _nt_dot(qh, kf.astype(BF16)) * dec_ref[h]
        st = st_ref[h]
        o = _dot(s.astype(BF16), vh) + _dot(qh, st.astype(BF16)) * wq_ref[h]
        kw_t = (kf * wk_ref[h]).T.astype(BF16)
        st_ref[h] = gc_ref[h] * st + _dot(kw_t, vh)
        mu = jnp.mean(o, axis=-1, keepdims=True)
        d = o - mu
        var = jnp.mean(d * d, axis=-1, keepdims=True)
        on = d * lax.rsqrt(var + GN_EPS)
        o_ref[0, :, sl] = (gate[:, sl] * (on * gn[:, sl])).astype(BF16)


def _ret_tables(seq, chunk):
    half = RET_HEAD_DIM // 2
    inv = ROPE_BASE ** (-jnp.arange(half, dtype=F32) / half)
    ang = jnp.arange(seq, dtype=jnp.int32).astype(F32)[:, None] * inv[None, :]
    cos, sin = jnp.cos(ang), jnp.sin(ang)
    zero = jnp.zeros_like(sin)
    cos_t = jnp.tile(cos, (1, 2 * RET_HEADS))
    sa = jnp.tile(jnp.concatenate([-sin, zero], axis=1), (1, RET_HEADS))
    sb = jnp.tile(jnp.concatenate([zero, sin], axis=1), (1, RET_HEADS))
    gamma = 1.0 - 2.0 ** (-5.0 - jnp.arange(RET_HEADS, dtype=F32))
    log_g = jnp.log(gamma)
    i = jnp.arange(chunk, dtype=F32)
    diff = i[:, None] - i[None, :]
    decay = jnp.where(diff >= 0, jnp.exp(log_g[:, None, None] * jnp.maximum(diff, 0.0)), 0.0)
    ones = jnp.ones((1, 1, RET_HEAD_DIM), F32)
    wq = jnp.exp(log_g[:, None] * (i + 1.0)[None, :])[:, :, None] * ones
    wk = jnp.exp(log_g[:, None] * (chunk - 1.0 - i)[None, :])[:, :, None] * ones
    gc = jnp.exp(log_g * chunk)[:, None, None] * jnp.ones((1, RET_HEAD_DIM, RET_HEAD_DIM), F32)
    return cos_t, sa, sb, decay, wq, wk, gc


def _retention(main, tables, gn):
    bsz, seq, _ = main.shape
    cos_t, sa, sb, decay, wq, wk, gc = tables
    c = decay.shape[-1]
    blk = lambda col: pl.BlockSpec((1, c, RET_WIDTH), lambda b, i: (b, i, col // RET_WIDTH))
    tab = pl.BlockSpec((c, RET_WIDTH), lambda b, i: (i, 0))
    const = lambda shape: pl.BlockSpec(shape, lambda b, i: (0,) * len(shape))
    return pl.pallas_call(
        _ret_kernel,
        grid=(bsz, seq // c),
        in_specs=[
            blk(COL_RQ), blk(COL_RK), blk(COL_RV), blk(COL_RG), tab, tab, tab,
            const((RET_HEADS, c, c)), const((RET_HEADS, c, RET_HEAD_DIM)),
            const((RET_HEADS, c, RET_HEAD_DIM)),
            const((RET_HEADS, RET_HEAD_DIM, RET_HEAD_DIM)), const((1, RET_WIDTH)),
        ],
        out_specs=pl.BlockSpec((1, c, RET_WIDTH), lambda b, i: (b, i, 0)),
        out_shape=jax.ShapeDtypeStruct((bsz, seq, RET_WIDTH), BF16),
        scratch_shapes=[pltpu.VMEM((RET_HEADS, RET_HEAD_DIM, RET_HEAD_DIM), F32)],
        compiler_params=pltpu.CompilerParams(
            dimension_semantics=("arbitrary", "arbitrary"), vmem_limit_bytes=VMEM_LIMIT),
        name="retention",
    )(main, main, main, main, cos_t, sa, sb, decay, wq, wk, gc, gn)


def _ffn_kernel(x_ref, ya_ref, yb_ref, yc_ref, wo_ref, g_ref, wg_ref, wu_ref, wd_ref, gf_ref,
                o_ref, x1_s, h_s, acc_s, *, final):
    x1 = (x_ref[0]
          + _dot(ya_ref[...].astype(BF16), wo_ref[0:S5_WIDTH, :])
          + _dot(yb_ref[0], wo_ref[S5_WIDTH:S5_WIDTH + DSA_WIDTH, :])
          + _dot(yc_ref[0], wo_ref[S5_WIDTH + DSA_WIDTH:, :]))
    x1_s[...] = x1
    ms = jnp.mean(x1 * x1, axis=-1, keepdims=True)
    h_s[...] = (x1 * lax.rsqrt(ms + RMS_EPS) * g_ref[...]).astype(BF16)
    acc_s[...] = jnp.zeros_like(acc_s)

    def chunk(c, carry):
        h = h_s[...]
        gate = _dot(h, wg_ref[c])
        up = _dot(h, wu_ref[c])
        a = (gate * jax.nn.sigmoid(gate) * up).astype(BF16)
        acc_s[...] += _dot(a, wd_ref[c])
        return carry

    lax.fori_loop(0, N_FF_CHUNK, chunk, 0)
    y = x1_s[...] + acc_s[...]
    if final:
        ms = jnp.mean(y * y, axis=-1, keepdims=True)
        y = y * lax.rsqrt(ms + RMS_EPS) * gf_ref[...]
    o_ref[0] = y


def _ffn(x, ya, yb, yc, wo, g, wg, wu, wd, gf, final):
    bsz, seq, _ = x.shape
    tm = min(TM_FFN, seq)
    once = pl.Buffered(1)
    const = lambda shape: pl.BlockSpec(shape, lambda b, i: (0,) * len(shape), pipeline_mode=once)
    return pl.pallas_call(
        functools.partial(_ffn_kernel, final=final),
        grid=(bsz, seq // tm),
        in_specs=[
            pl.BlockSpec((1, tm, D_MODEL), lambda b, i: (b, i, 0)),
            pl.BlockSpec((tm, S5_WIDTH), lambda b, i: (i, b)),
            pl.BlockSpec((1, tm, DSA_WIDTH), lambda b, i: (b, i, 0)),
            pl.BlockSpec((1, tm, RET_WIDTH), lambda b, i: (b, i, 0)),
            const((D_MODEL, D_MODEL)), const((1, D_MODEL)),
            const((N_FF_CHUNK, D_MODEL, FF_CHUNK)), const((N_FF_CHUNK, D_MODEL, FF_CHUNK)),
            const((N_FF_CHUNK, FF_CHUNK, D_MODEL)), const((1, D_MODEL)),
        ],
        out_specs=pl.BlockSpec((1, tm, D_MODEL), lambda b, i: (b, i, 0)),
        out_shape=jax.ShapeDtypeStruct((bsz, seq, D_MODEL), F32),
        scratch_shapes=[pltpu.VMEM((tm, D_MODEL), F32), pltpu.VMEM((tm, D_MODEL), BF16),
                        pltpu.VMEM((tm, D_MODEL), F32)],
        compiler_params=pltpu.CompilerParams(
            dimension_semantics=("arbitrary", "arbitrary"), vmem_limit_bytes=VMEM_LIMIT),
        name="outproj_ffn",
    )(x, ya, yb, yc, wo, g, wg, wu, wd, gf)


def _pack_w_in(w_in):
    pts = [int(s) for s in np.cumsum(IN_SPLITS)[:-1]]
    u, dq, dk, dv, iq, ik, iw, rq, rk, rv, rg = jnp.split(w_in, pts, axis=-1)
    z = lambda n: jnp.zeros(w_in.shape[:-1] + (n,), w_in.dtype)
    parts = [dq, iq, rq, rk, rv, rg, dk, z(64), dv, z(64), ik, iw, z(60), z(128), u]
    return jnp.concatenate(parts, axis=-1).astype(BF16)


def _s5_params(a_re, a_im, log_dt, b_re, b_im, c_re, c_im, d_skip):
    g, n, p = S5_GROUPS, S5_STATE, S5_GROUP
    dt = jnp.exp(log_dt)[:, None]
    mag = jnp.exp(a_re * dt)
    ang = a_im * dt
    abr, abi = mag * jnp.cos(ang), mag * jnp.sin(ang)
    den = a_re * a_re + a_im * a_im
    nr, ni = abr - 1.0, abi
    cr = (nr * a_re + ni * a_im) / den
    ci = (ni * a_re - nr * a_im) / den
    bbr = cr[..., None] * b_re - ci[..., None] * b_im
    bbi = cr[..., None] * b_im + ci[..., None] * b_re
    eye = jnp.eye(g, dtype=F32)

    def b_block(bb):
        return (jnp.swapaxes(bb, 1, 2)[:, :, None, :] * eye[:, None, :, None]).reshape(g * p, g * n)

    def c_block(cc):
        return (jnp.swapaxes(cc, 1, 2)[:, :, None, :] * eye[:, None, :, None]).reshape(g * n, g * p)

    bmat = jnp.concatenate([b_block(bbr), b_block(bbi)], axis=1).astype(BF16)
    cmat = jnp.concatenate([c_block(c_re), -c_block(c_im)], axis=0).astype(BF16)
    return bmat, cmat, abr.reshape(1, g * n), abi.reshape(1, g * n), d_skip.reshape(1, g * p)


def _tri_lower():
    j = np.arange(K_BLK)
    return jnp.asarray((j[None, :] <= j[:, None]).astype(np.float32), BF16)


def _stacked_eye():
    return jnp.asarray(np.tile(np.eye(Q_BLK, dtype=np.float32), (DSA_HEADS, 1)), BF16)


def _far_bias_cols(rel_bias):
    c = rel_bias[REL_BUCKETS - 1, :].astype(F32)
    hi = c.astype(BF16)
    lo = (c - hi.astype(F32)).astype(BF16)
    cols = jnp.zeros((DSA_HEADS, Q_BLK, DSA_HEAD_DIM), BF16)
    cols = cols.at[:, :, 0].set(hi[:, None])
    return cols.at[:, :, 1].set(lo[:, None])


def kernel(x, w_in, w_out, norm_mix, ssm_a_re, ssm_a_im, ssm_log_dt, ssm_b_re, ssm_b_im,
           ssm_c_re, ssm_c_im, ssm_d, ssm_glu_w, ssm_glu_b, rel_bias, ret_gn, norm_ffn,
           w_ffn_in, w_ffn_out, norm_final):
    bsz, seq, _ = x.shape
    depth = w_in.shape[0]
    w_all = _pack_w_in(w_in)
    wo = w_out.astype(BF16)
    wg = w_ffn_in[:, :, :D_FF].reshape(depth, D_MODEL, N_FF_CHUNK, FF_CHUNK)
    wu = w_ffn_in[:, :, D_FF:].reshape(depth, D_MODEL, N_FF_CHUNK, FF_CHUNK)
    wg = jnp.swapaxes(wg, 1, 2).astype(BF16)
    wu = jnp.swapaxes(wu, 1, 2).astype(BF16)
    wd = w_ffn_out.reshape(depth, N_FF_CHUNK, FF_CHUNK, D_MODEL).astype(BF16)
    glu_w = ssm_glu_w.astype(BF16)

    bias = _bias_tiles(rel_bias)
    tri = _tri_lower()
    eye = _stacked_eye()
    cfar = _far_bias_cols(rel_bias)
    tables = _ret_tables(seq, min(RET_CHUNK, seq))

    for l in range(depth):
        main, idxf, u_t = _inproj(x, norm_mix[l][None, :], w_all[l])
        s5p = _s5_params(ssm_a_re[l], ssm_a_im[l], ssm_log_dt[l], ssm_b_re[l], ssm_b_im[l],
                         ssm_c_re[l], ssm_c_im[l], ssm_d[l])
        ya = _s5(u_t.reshape(seq, bsz, S5_WIDTH), *s5p, glu_w[l], ssm_glu_b[l][None, :])
        yb = _dsa(main, idxf, bias, tri, eye, cfar)
        yc = _retention(main, tables, ret_gn[l][None, :])
        x = _ffn(x, ya.reshape(seq, bsz * S5_WIDTH), yb, yc, wo[l], norm_ffn[l][None, :],
                 wg[l], wu[l], wd[l], norm_final[None, :], final=(l == depth - 1))
    return x
```

```python
import functools
import math

import numpy as np
import jax
import jax.numpy as jnp
from jax import lax
from jax.experimental import pallas as pl
from jax.experimental.pallas import tpu as pltpu

F32 = jnp.float32
BF16 = jnp.bfloat16
I32 = jnp.int32
I16 = jnp.int16

D_MODEL = 1024
DEPTH = 4
S5_WIDTH = 256
S5_GROUP = 16
S5_GROUPS = 16
S5_STATE = 64
DSA_HEADS = 8
DSA_HEAD_DIM = 64
DSA_WIDTH = 512
DSA_KV_DIM = 64
IDX_HEADS = 4
IDX_DIM = 64
DSA_TOPK_MAX = 256
RET_HEADS = 4
RET_HEAD_DIM = 64
RET_WIDTH = 256
ROPE_BASE = 10000.0
REL_BUCKETS = 32
REL_MAX_DIST = 128
D_FF = 2816
IN_SPLITS = (256, 512, 64, 64, 256, 64, 4, 256, 256, 256, 256)
RMS_EPS = 1e-6
GN_EPS = 1e-6

COL_DQ = 0
COL_IQ = 512
COL_RQ = 768
COL_RK = 1024
COL_RV = 1280
COL_RG = 1536
COL_KG = 1792
COL_VG = 1920
COL_IX = 2048
MAIN_W = 2304
ONES_COLS = (COL_KG + 64, COL_KG + 65, COL_VG + 64)
PROJ_W = MAIN_W + S5_WIDTH
PROJ_CHUNK = 256

LANES = 128
Q_BLK = 128
Q_PAIR = 2
K_BLK = 256
NEG_BIG = -1e30
INT_MIN = -2147483648
HALF16 = 32768
KEY_NEG_INF = -2139095041
FF_CHUNK = 256
N_FF_CHUNK = D_FF // FF_CHUNK
RET_CHUNK = 512
S5_TL = 256
TM_PROJ = 1024
TM_FFN = 1024
VMEM_LIMIT = 56 * 1024 * 1024


def _nt_dot(a, b):
    return lax.dot_general(a, b, (((1,), (1,)), ((), ())), preferred_element_type=F32)


def _dot(a, b):
    return jnp.dot(a, b, preferred_element_type=F32)


def _inproj_kernel(x_ref, g_ref, w_ref, main_ref, idx_ref, u_ref):
    x = x_ref[0]
    ms = jnp.mean(x * x, axis=-1, keepdims=True)
    h = (x * lax.rsqrt(ms + RMS_EPS) * g_ref[...]).astype(BF16)
    n_main = MAIN_W // PROJ_CHUNK
    for c in range(n_main + 1):
        acc = _dot(h, w_ref[:, c * PROJ_CHUNK:(c + 1) * PROJ_CHUNK])
        if c == n_main:
            u_ref[...] = acc
            continue
        for one_col in ONES_COLS:
            if c * PROJ_CHUNK <= one_col < (c + 1) * PROJ_CHUNK:
                lane = lax.broadcasted_iota(I32, acc.shape, 1)
                acc = jnp.where(lane == one_col - c * PROJ_CHUNK, 1.0, acc)
        if c * PROJ_CHUNK == COL_IX:
            idx_ref[0] = acc[:, :LANES]
        main_ref[0, :, c * PROJ_CHUNK:(c + 1) * PROJ_CHUNK] = acc.astype(BF16)


def _inproj(x, g, w_all):
    bsz, seq, _ = x.shape
    tm = min(TM_PROJ, seq)
    return pl.pallas_call(
        _inproj_kernel,
        grid=(bsz, seq // tm),
        in_specs=[
            pl.BlockSpec((1, tm, D_MODEL), lambda b, i: (b, i, 0)),
            pl.BlockSpec((1, D_MODEL), lambda b, i: (0, 0)),
            pl.BlockSpec((D_MODEL, PROJ_W), lambda b, i: (0, 0)),
        ],
        out_specs=[
            pl.BlockSpec((1, tm, MAIN_W), lambda b, i: (b, i, 0)),
            pl.BlockSpec((1, tm, LANES), lambda b, i: (b, i, 0)),
            pl.BlockSpec((tm, S5_WIDTH), lambda b, i: (i, b)),
        ],
        out_shape=[
            jax.ShapeDtypeStruct((bsz, seq, MAIN_W), BF16),
            jax.ShapeDtypeStruct((bsz, seq, LANES), F32),
            jax.ShapeDtypeStruct((seq, bsz * S5_WIDTH), F32),
        ],
        compiler_params=pltpu.CompilerParams(
            dimension_semantics=("arbitrary", "arbitrary"), vmem_limit_bytes=VMEM_LIMIT),
        name="inproj",
    )(x, g, w_all)


def _s5_kernel(u_ref, bmat_ref, cmat_ref, ar_ref, ai_ref, d_ref, gw_ref, gb_ref, o_ref,
               xs_ref, h_ref):
    tl, bsz, _ = u_ref.shape
    nst = S5_GROUPS * S5_STATE

    @pl.when(pl.program_id(0) == 0)
    def _():
        h_ref[...] = jnp.zeros_like(h_ref)

    u = u_ref[...].reshape(tl * bsz, S5_WIDTH)
    xs_ref[...] = _dot(u.astype(BF16), bmat_ref[...])
    ar = jnp.broadcast_to(ar_ref[...], (bsz, nst))
    ai = jnp.broadcast_to(ai_ref[...], (bsz, nst))

    def step(t, carry):
        hr, hi = carry
        rows = pl.ds(pl.multiple_of(t * bsz, bsz), bsz)
        nr = ar * hr - ai * hi + xs_ref[rows, 0:nst]
        ni = ar * hi + ai * hr + xs_ref[rows, nst:2 * nst]
        xs_ref[rows, 0:nst] = nr
        xs_ref[rows, nst:2 * nst] = ni
        return nr, ni

    hr, hi = lax.fori_loop(0, tl, step, (h_ref[:, 0:nst], h_ref[:, nst:2 * nst]), unroll=4)
    h_ref[:, 0:nst] = hr
    h_ref[:, nst:2 * nst] = hi

    y = _dot(xs_ref[...].astype(BF16), cmat_ref[...]) + d_ref[...] * u
    y = jax.nn.gelu(y)
    z = _dot(y.astype(BF16), gw_ref[...]) + gb_ref[...]
    o_ref[...] = (y * jax.nn.sigmoid(z)).reshape(tl, bsz, S5_WIDTH)


def _s5(u_t, bmat, cmat, ar, ai, dvec, gw, gb):
    seq, bsz, _ = u_t.shape
    tl = min(S5_TL, seq)
    nst2 = 2 * S5_GROUPS * S5_STATE
    const = lambda shape: pl.BlockSpec(shape, lambda i: (0,) * len(shape))
    return pl.pallas_call(
        _s5_kernel,
        grid=(seq // tl,),
        in_specs=[
            pl.BlockSpec((tl, bsz, S5_WIDTH), lambda i: (i, 0, 0)),
            const((S5_WIDTH, nst2)), const((nst2, S5_WIDTH)),
            const((1, nst2 // 2)), const((1, nst2 // 2)), const((1, S5_WIDTH)),
            const((S5_WIDTH, S5_WIDTH)), const((1, S5_WIDTH)),
        ],
        out_specs=pl.BlockSpec((tl, bsz, S5_WIDTH), lambda i: (i, 0, 0)),
        out_shape=jax.ShapeDtypeStruct((seq, bsz, S5_WIDTH), F32),
        scratch_shapes=[pltpu.VMEM((tl * bsz, nst2), F32), pltpu.VMEM((bsz, nst2), F32)],
        compiler_params=pltpu.CompilerParams(
            dimension_semantics=("arbitrary",), vmem_limit_bytes=VMEM_LIMIT),
        name="s5_scan",
    )(u_t, bmat, cmat, ar, ai, dvec, gw, gb)


def _bias_kernel(rb_ref, bk_ref, o_ref):
    h = pl.program_id(1)
    bk = bk_ref[0]
    acc = jnp.zeros(bk.shape, F32)
    for b in range(REL_BUCKETS):
        acc = jnp.where(bk == b, rb_ref[b, h], acc)
    o_ref[0] = acc - rb_ref[REL_BUCKETS - 1, h]


def _bucket_tiles():
    d = np.arange(4, dtype=np.int64)[:, None, None] * Q_BLK
    i = np.arange(Q_BLK, dtype=np.int64)[None, :, None]
    j = np.arange(K_BLK, dtype=np.int64)[None, None, :]
    n = np.maximum(d + i - j, 0)
    max_exact = REL_BUCKETS // 2
    nf = np.maximum(n, 1).astype(np.float32)
    large = max_exact + (np.log(nf / np.float32(max_exact))
                         / np.float32(math.log(REL_MAX_DIST / max_exact))
                         * np.float32(REL_BUCKETS - max_exact)).astype(np.int32)
    large = np.minimum(large, REL_BUCKETS - 1)
    return np.where(n < max_exact, n, large).astype(np.int32)


def _bias_tiles(rel_bias):
    buckets = jnp.asarray(_bucket_tiles())
    return pl.pallas_call(
        _bias_kernel,
        grid=(4, DSA_HEADS),
        in_specs=[
            pl.BlockSpec(memory_space=pltpu.SMEM),
            pl.BlockSpec((1, Q_BLK, K_BLK), lambda d, h: (d, 0, 0)),
        ],
        out_specs=pl.BlockSpec((1, Q_BLK, K_BLK), lambda d, h: (d, h, 0)),
        out_shape=jax.ShapeDtypeStruct((4, DSA_HEADS * Q_BLK, K_BLK), F32),
        name="rel_bias_tiles",
    )(rel_bias, buckets)


def _dsa_kernel(dq_ref, iq_ref, iw_ref, ix_ref, kg_ref, vg_ref, bias_ref, tri_ref, eye_ref,
                cfar_ref, y_ref, keys_s, kh_s, kl_s, qs_s, iqs_s, t_s, r_s, m_s, acc_s, er_s,
                *, k_sel):
    qp = pl.program_id(1)
    nkb = qp + 1
    sub = K_BLK // 8
    rows = lambda p: slice(p * Q_BLK, (p + 1) * Q_BLK)
    hrows = lambda h: slice(h * Q_BLK, (h + 1) * Q_BLK)

    w_rows = []
    for p in range(Q_PAIR):
        qs_s[p, :, 0:Q_BLK] = eye_ref[...]
        for h in range(DSA_HEADS):
            dq = dq_ref[0, rows(p), h * DSA_HEAD_DIM:(h + 1) * DSA_HEAD_DIM]
            qs_s[p, hrows(h), Q_BLK:] = jnp.concatenate(
                [dq * jnp.asarray(0.125, BF16), cfar_ref[h]], axis=1)
        for h in range(IDX_HEADS):
            iqs_s[p * IDX_HEADS * Q_BLK + h * Q_BLK:p * IDX_HEADS * Q_BLK + (h + 1) * Q_BLK, :] = (
                iq_ref[0, rows(p), h * IDX_DIM:(h + 1) * IDX_DIM])
        w_t = iw_ref[0, rows(p), :].T
        w_rows.append(jnp.concatenate(
            [w_t[IDX_DIM + h:IDX_DIM + h + 1, :] for h in range(IDX_HEADS)], axis=1) * 0.0625)
    w_row = jnp.concatenate(w_rows, axis=1)

    key_pos = lax.broadcasted_iota(I32, (K_BLK, Q_BLK), 0)
    lane_pos = lax.broadcasted_iota(I32, (K_BLK, Q_BLK), 1)

    def score_block(kb, carry):
        ks = pl.multiple_of(kb * K_BLK, K_BLK)
        ik = ix_ref[0, pl.ds(ks, K_BLK), 0:IDX_DIM]
        r = jnp.maximum(_nt_dot(ik, iqs_s[...]), 0.0) * w_row
        for p in range(Q_PAIR):
            c0 = p * IDX_HEADS * Q_BLK
            s = ((r[:, c0:c0 + Q_BLK] + r[:, c0 + Q_BLK:c0 + 2 * Q_BLK])
                 + (r[:, c0 + 2 * Q_BLK:c0 + 3 * Q_BLK] + r[:, c0 + 3 * Q_BLK:c0 + 4 * Q_BLK]))
            s = jnp.where(s == 0.0, 0.0, s)
            qry_pos = (Q_PAIR * qp + p) * Q_BLK + lane_pos
            s = jnp.where(key_pos + ks <= qry_pos, s, -jnp.inf)
            bits = lax.bitcast_convert_type(s, I32)
            key = bits ^ ((bits >> 31) & 0x7FFFFFFF)
            keys_s[p, pl.ds(ks, K_BLK), :] = key
            kh_s[p, pl.ds(ks, K_BLK), :] = (key >> 16).astype(I16)
            kl_s[p, pl.ds(ks, K_BLK), :] = ((key & 0xFFFF) - HALF16).astype(I16)
        return carry

    lax.fori_loop(0, nkb, score_block, 0)

    t_s[...] = jnp.full(t_s.shape, KEY_NEG_INF, I32)
    r_s[...] = jnp.zeros_like(r_s)

    @pl.when(qp * Q_PAIR * Q_BLK >= k_sel)
    def _():
        sub16 = K_BLK // 16

        def count16(ref, cand):
            c16 = jnp.concatenate([cand, cand], axis=1).astype(I16)[:, None, None]
            def body(kb, acc):
                ks = pl.multiple_of(kb * K_BLK, K_BLK)
                blk = ref[:, pl.ds(ks, K_BLK), :].reshape(Q_PAIR, sub16 // 4, 4, 16, Q_BLK)
                one = jnp.ones(blk.shape, I16)
                hit = jnp.where(blk >= c16, one, jnp.zeros_like(one))
                for j in range(sub16 // 4):
                    acc = acc + hit[:, j]
                return acc
            acc = lax.fori_loop(0, nkb, body, jnp.zeros((Q_PAIR, 4, 16, Q_BLK), I16))
            tot = jnp.sum(jnp.sum(acc.astype(I32), axis=1), axis=1, keepdims=True)
            return jnp.broadcast_to(tot, (Q_PAIR, 8, Q_BLK))

        def bisect16(ref, target):
            def bit_step(i, tu):
                cand_u = tu | lax.shift_left(jnp.int32(1), 15 - i)
                return jnp.where(count16(ref, cand_u - HALF16) >= target, cand_u, tu)
            return lax.fori_loop(0, 16, bit_step, jnp.zeros((Q_PAIR, 8, Q_BLK), I32))

        hi = bisect16(kh_s, k_sel) - HALF16
        need = k_sel - count16(kh_s, hi + 1)
        hi16 = jnp.concatenate([hi, hi], axis=1).astype(I16)

        def keep_equal_high(kb, carry):
            ks = pl.multiple_of(kb * K_BLK, K_BLK)
            same = kh_s[:, pl.ds(ks, K_BLK), :].reshape(Q_PAIR, sub16, 16, Q_BLK) == hi16[:, None]
            low = kl_s[:, pl.ds(ks, K_BLK), :].reshape(Q_PAIR, sub16, 16, Q_BLK)
            kl_s[:, pl.ds(ks, K_BLK), :] = jnp.where(
                same, low, jnp.full(low.shape, -HALF16, I16)).reshape(Q_PAIR, K_BLK, Q_BLK)
            return carry

        lax.fori_loop(0, nkb, keep_equal_high, 0)
        tk = hi * (2 * HALF16) + bisect16(kl_s, need)

        def count_gt(kb, acc):
            ks = pl.multiple_of(kb * K_BLK, K_BLK)
            blk = keys_s[:, pl.ds(ks, K_BLK), :].reshape(Q_PAIR, sub // 4, 4, 8, Q_BLK)
            return acc + jnp.sum(jnp.where(blk > tk[:, None, None], 1.0, 0.0), axis=1)

        acc = lax.fori_loop(0, nkb, count_gt, jnp.zeros((Q_PAIR, 4, 8, Q_BLK), F32))
        t_s[...] = tk
        r_s[...] = k_sel - jnp.broadcast_to(
            jnp.sum(jnp.sum(acc, axis=1), axis=1, keepdims=True), (Q_PAIR, 8, Q_BLK))

    m_s[...] = jnp.full(m_s.shape, NEG_BIG, F32)
    acc_s[...] = jnp.zeros_like(acc_s)
    er_s[...] = jnp.zeros_like(er_s)

    def attn_block(kb, carry, near):
        ks = pl.multiple_of(kb * K_BLK, K_BLK)
        kgb = kg_ref[0, pl.ds(ks, K_BLK), :]
        vgb = vg_ref[0, pl.ds(ks, K_BLK), :]
        for p in range(Q_PAIR):
            t8, r8 = t_s[p], r_s[p]
            key = keys_s[p, pl.ds(ks, K_BLK), :].reshape(sub, 8, Q_BLK)
            eq = key == t8
            pre = _dot(tri_ref[...], jnp.where(eq, 1.0, 0.0).reshape(K_BLK, Q_BLK).astype(BF16))
            er = er_s[p]
            take = pre.reshape(sub, 8, Q_BLK) + er <= r8
            pen_t = jnp.where(key > t8, 0.0,
                              jnp.where(eq, jnp.where(take, 0.0, NEG_BIG), NEG_BIG))
            er_s[p] = er + jnp.broadcast_to(pre[K_BLK - 1:K_BLK, :], (8, Q_BLK))

            rhs = jnp.concatenate([pen_t.reshape(K_BLK, Q_BLK).astype(BF16), kgb], axis=1)
            s = _nt_dot(qs_s[p], rhs)
            if near:
                s = s + bias_ref[jnp.minimum(Q_PAIR * qp + p - 2 * kb, 3)]
            m_old = m_s[p]
            m_new = jnp.maximum(m_old, jnp.max(s, axis=1, keepdims=True))
            pr = jnp.exp(s - jnp.concatenate([m_new, m_new], axis=1))
            pv = _dot(pr.astype(BF16), vgb)
            acc_s[p] = jnp.exp(m_old - m_new) * acc_s[p] + pv
            m_s[p] = m_new
        return carry

    n_far = jnp.maximum(nkb - 2, 0)
    lax.fori_loop(0, n_far, functools.partial(attn_block, near=False), 0)
    lax.fori_loop(n_far, nkb, functools.partial(attn_block, near=True), 0)

    for p in range(Q_PAIR):
        acc = acc_s[p]
        o = acc[:, 0:DSA_HEAD_DIM] / acc[:, DSA_HEAD_DIM:DSA_HEAD_DIM + 1]
        for h in range(DSA_HEADS):
            y_ref[0, rows(p), h * DSA_HEAD_DIM:(h + 1) * DSA_HEAD_DIM] = o[hrows(h)].astype(BF16)


def _dsa(main, idxf, bias, tri, eye, cfar):
    bsz, seq, _ = main.shape
    k_sel = min(DSA_TOPK_MAX, seq // 4)
    qrows = Q_PAIR * Q_BLK
    assert qrows == K_BLK and seq % K_BLK == 0 and k_sel % qrows == 0
    col = lambda c, w: c // w
    return pl.pallas_call(
        functools.partial(_dsa_kernel, k_sel=k_sel),
        grid=(bsz, seq // qrows),
        in_specs=[
            pl.BlockSpec((1, qrows, DSA_WIDTH), lambda b, q: (b, q, col(COL_DQ, DSA_WIDTH))),
            pl.BlockSpec((1, qrows, 256), lambda b, q: (b, q, col(COL_IQ, 256))),
            pl.BlockSpec((1, qrows, LANES), lambda b, q: (b, q, 0)),
            pl.BlockSpec((1, seq, LANES), lambda b, q: (b, 0, col(COL_IX, LANES))),
            pl.BlockSpec((1, seq, LANES), lambda b, q: (b, 0, col(COL_KG, LANES))),
            pl.BlockSpec((1, seq, LANES), lambda b, q: (b, 0, col(COL_VG, LANES))),
            pl.BlockSpec((4, DSA_HEADS * Q_BLK, K_BLK), lambda b, q: (0, 0, 0)),
            pl.BlockSpec((K_BLK, K_BLK), lambda b, q: (0, 0)),
            pl.BlockSpec((DSA_HEADS * Q_BLK, Q_BLK), lambda b, q: (0, 0)),
            pl.BlockSpec((DSA_HEADS, Q_BLK, DSA_HEAD_DIM), lambda b, q: (0, 0, 0)),
        ],
        out_specs=pl.BlockSpec((1, qrows, DSA_WIDTH), lambda b, q: (b, q, 0)),
        out_shape=jax.ShapeDtypeStruct((bsz, seq, DSA_WIDTH), BF16),
        scratch_shapes=[
            pltpu.VMEM((Q_PAIR, seq, Q_BLK), I32),
            pltpu.VMEM((Q_PAIR, seq, Q_BLK), I16),
            pltpu.VMEM((Q_PAIR, seq, Q_BLK), I16),
            pltpu.VMEM((Q_PAIR, DSA_HEADS * Q_BLK, 2 * Q_BLK), BF16),
            pltpu.VMEM((Q_PAIR * IDX_HEADS * Q_BLK, IDX_DIM), BF16),
            pltpu.VMEM((Q_PAIR, 8, Q_BLK), I32),
            pltpu.VMEM((Q_PAIR, 8, Q_BLK), F32),
            pltpu.VMEM((Q_PAIR, DSA_HEADS * Q_BLK, LANES), F32),
            pltpu.VMEM((Q_PAIR, DSA_HEADS * Q_BLK, LANES), F32),
            pltpu.VMEM((Q_PAIR, 8, Q_BLK), F32),
        ],
        compiler_params=pltpu.CompilerParams(
            dimension_semantics=("arbitrary", "arbitrary"), vmem_limit_bytes=VMEM_LIMIT),
        name="dsa_attention",
    )(main, main, idxf, main, main, main, bias, tri, eye, cfar)


def _ret_kernel(q_ref, k_ref, v_ref, g_ref, cos_ref, sa_ref, sb_ref, dec_ref, wq_ref, wk_ref,
                gc_ref, gn_ref, o_ref, st_ref):
    @pl.when(pl.program_id(1) == 0)
    def _():
        st_ref[...] = jnp.zeros_like(st_ref)

    cos, sa, sb = cos_ref[...], sa_ref[...], sb_ref[...]
    half = RET_HEAD_DIM // 2

    def rot(x):
        return (x * cos + pltpu.roll(x, RET_WIDTH - half, 1) * sa + pltpu.roll(x, half, 1) * sb)

    q = rot(q_ref[0].astype(F32))
    k = rot(k_ref[0].astype(F32)) * (RET_HEAD_DIM ** -0.5)
    v = v_ref[0]
    g = g_ref[0].astype(F32)
    gate = g * jax.nn.sigmoid(g)
    gn = gn_ref[...]
    for h in range(RET_HEADS):
        sl = slice(h * RET_HEAD_DIM, (h + 1) * RET_HEAD_DIM)
        qh = q[:, sl].astype(BF16)
        kf = k[:, sl]
        vh = v[:, sl]
        s = _nt_dot(qh, kf.astype(BF16)) * dec_ref[h]
        st = st_ref[h]
        o = _dot(s.astype(BF16), vh) + _dot(qh, st.astype(BF16)) * wq_ref[h]
        kw_t = (kf * wk_ref[h]).T.astype(BF16)
        st_ref[h] = gc_ref[h] * st + _dot(kw_t, vh)
        mu = jnp.mean(o, axis=-1, keepdims=True)
        d = o - mu
        var = jnp.mean(d * d, axis=-1, keepdims=True)
        on = d * lax.rsqrt(var + GN_EPS)
        o_ref[0, :, sl] = (gate[:, sl] * (on * gn[:, sl])).astype(BF16)


def _ret_tables(seq, chunk):
    half = RET_HEAD_DIM // 2
    inv = ROPE_BASE ** (-jnp.arange(half, dtype=F32) / half)
    ang = jnp.arange(seq, dtype=jnp.int32).astype(F32)[:, None] * inv[None, :]
    cos, sin = jnp.cos(ang), jnp.sin(ang)
    zero = jnp.zeros_like(sin)
    cos_t = jnp.tile(cos, (1, 2 * RET_HEADS))
    sa = jnp.tile(jnp.concatenate([-sin, zero], axis=1), (1, RET_HEADS))
    sb = jnp.tile(jnp.concatenate([zero, sin], axis=1), (1, RET_HEADS))
    gamma = 1.0 - 2.0 ** (-5.0 - jnp.arange(RET_HEADS, dtype=F32))
    log_g = jnp.log(gamma)
    i = jnp.arange(chunk, dtype=F32)
    diff = i[:, None] - i[None, :]
    decay = jnp.where(diff >= 0, jnp.exp(log_g[:, None, None] * jnp.maximum(diff, 0.0)), 0.0)
    ones = jnp.ones((1, 1, RET_HEAD_DIM), F32)
    wq = jnp.exp(log_g[:, None] * (i + 1.0)[None, :])[:, :, None] * ones
    wk = jnp.exp(log_g[:, None] * (chunk - 1.0 - i)[None, :])[:, :, None] * ones
    gc = jnp.exp(log_g * chunk)[:, None, None] * jnp.ones((1, RET_HEAD_DIM, RET_HEAD_DIM), F32)
    return cos_t, sa, sb, decay, wq, wk, gc


def _retention(main, tables, gn):
    bsz, seq, _ = main.shape
    cos_t, sa, sb, decay, wq, wk, gc = tables
    c = decay.shape[-1]
    blk = lambda col: pl.BlockSpec((1, c, RET_WIDTH), lambda b, i: (b, i, col // RET_WIDTH))
    tab = pl.BlockSpec((c, RET_WIDTH), lambda b, i: (i, 0))
    const = lambda shape: pl.BlockSpec(shape, lambda b, i: (0,) * len(shape))
    return pl.pallas_call(
        _ret_kernel,
        grid=(bsz, seq // c),
        in_specs=[
            blk(COL_RQ), blk(COL_RK), blk(COL_RV), blk(COL_RG), tab, tab, tab,
            const((RET_HEADS, c, c)), const((RET_HEADS, c, RET_HEAD_DIM)),
            const((RET_HEADS, c, RET_HEAD_DIM)),
            const((RET_HEADS, RET_HEAD_DIM, RET_HEAD_DIM)), const((1, RET_WIDTH)),
        ],
        out_specs=pl.BlockSpec((1, c, RET_WIDTH), lambda b, i: (b, i, 0)),
        out_shape=jax.ShapeDtypeStruct((bsz, seq, RET_WIDTH), BF16),
        scratch_shapes=[pltpu.VMEM((RET_HEADS, RET_HEAD_DIM, RET_HEAD_DIM), F32)],
        compiler_params=pltpu.CompilerParams(
            dimension_semantics=("arbitrary", "arbitrary"), vmem_limit_bytes=VMEM_LIMIT),
        name="retention",
    )(main, main, main, main, cos_t, sa, sb, decay, wq, wk, gc, gn)


def _ffn_kernel(x_ref, ya_ref, yb_ref, yc_ref, wo_ref, g_ref, wg_ref, wu_ref, wd_ref, gf_ref,
                o_ref, x1_s, h_s, acc_s, *, final):
    x1 = (x_ref[0]
          + _dot(ya_ref[...].astype(BF16), wo_ref[0:S5_WIDTH, :])
          + _dot(yb_ref[0], wo_ref[S5_WIDTH:S5_WIDTH + DSA_WIDTH, :])
          + _dot(yc_ref[0], wo_ref[S5_WIDTH + DSA_WIDTH:, :]))
    x1_s[...] = x1
    ms = jnp.mean(x1 * x1, axis=-1, keepdims=True)
    h_s[...] = (x1 * lax.rsqrt(ms + RMS_EPS) * g_ref[...]).astype(BF16)
    acc_s[...] = jnp.zeros_like(acc_s)

    def chunk(c, carry):
        h = h_s[...]
        gate = _dot(h, wg_ref[c])
        up = _dot(h, wu_ref[c])
        a = (gate * jax.nn.sigmoid(gate) * up).astype(BF16)
        acc_s[...] += _dot(a, wd_ref[c])
        return carry

    lax.fori_loop(0, N_FF_CHUNK, chunk, 0)
    y = x1_s[...] + acc_s[...]
    if final:
        ms = jnp.mean(y * y, axis=-1, keepdims=True)
        y = y * lax.rsqrt(ms + RMS_EPS) * gf_ref[...]
    o_ref[0] = y


def _ffn(x, ya, yb, yc, wo, g, wg, wu, wd, gf, final):
    bsz, seq, _ = x.shape
    tm = min(TM_FFN, seq)
    once = pl.Buffered(1)
    const = lambda shape: pl.BlockSpec(shape, lambda b, i: (0,) * len(shape), pipeline_mode=once)
    return pl.pallas_call(
        functools.partial(_ffn_kernel, final=final),
        grid=(bsz, seq // tm),
        in_specs=[
            pl.BlockSpec((1, tm, D_MODEL), lambda b, i: (b, i, 0)),
            pl.BlockSpec((tm, S5_WIDTH), lambda b, i: (i, b)),
            pl.BlockSpec((1, tm, DSA_WIDTH), lambda b, i: (b, i, 0)),
            pl.BlockSpec((1, tm, RET_WIDTH), lambda b, i: (b, i, 0)),
            const((D_MODEL, D_MODEL)), const((1, D_MODEL)),
            const((N_FF_CHUNK, D_MODEL, FF_CHUNK)), const((N_FF_CHUNK, D_MODEL, FF_CHUNK)),
            const((N_FF_CHUNK, FF_CHUNK, D_MODEL)), const((1, D_MODEL)),
        ],
        out_specs=pl.BlockSpec((1, tm, D_MODEL), lambda b, i: (b, i, 0)),
        out_shape=jax.ShapeDtypeStruct((bsz, seq, D_MODEL), F32),
        scratch_shapes=[pltpu.VMEM((tm, D_MODEL), F32), pltpu.VMEM((tm, D_MODEL), BF16),
                        pltpu.VMEM((tm, D_MODEL), F32)],
        compiler_params=pltpu.CompilerParams(
            dimension_semantics=("arbitrary", "arbitrary"), vmem_limit_bytes=VMEM_LIMIT),
        name="outproj_ffn",
    )(x, ya, yb, yc, wo, g, wg, wu, wd, gf)


def _pack_w_in(w_in):
    pts = [int(s) for s in np.cumsum(IN_SPLITS)[:-1]]
    u, dq, dk, dv, iq, ik, iw, rq, rk, rv, rg = jnp.split(w_in, pts, axis=-1)
    z = lambda n: jnp.zeros(w_in.shape[:-1] + (n,), w_in.dtype)
    parts = [dq, iq, rq, rk, rv, rg, dk, z(64), dv, z(64), ik, iw, z(60), z(128), u]
    return jnp.concatenate(parts, axis=-1).astype(BF16)


def _s5_params(a_re, a_im, log_dt, b_re, b_im, c_re, c_im, d_skip):
    g, n, p = S5_GROUPS, S5_STATE, S5_GROUP
    dt = jnp.exp(log_dt)[:, None]
    mag = jnp.exp(a_re * dt)
    ang = a_im * dt
    abr, abi = mag * jnp.cos(ang), mag * jnp.sin(ang)
    den = a_re * a_re + a_im * a_im
    nr, ni = abr - 1.0, abi
    cr = (nr * a_re + ni * a_im) / den
    ci = (ni * a_re - nr * a_im) / den
    bbr = cr[..., None] * b_re - ci[..., None] * b_im
    bbi = cr[..., None] * b_im + ci[..., None] * b_re
    eye = jnp.eye(g, dtype=F32)

    def b_block(bb):
        return (jnp.swapaxes(bb, 1, 2)[:, :, None, :] * eye[:, None, :, None]).reshape(g * p, g * n)

    def c_block(cc):
        return (jnp.swapaxes(cc, 1, 2)[:, :, None, :] * eye[:, None, :, None]).reshape(g * n, g * p)

    bmat = jnp.concatenate([b_block(bbr), b_block(bbi)], axis=1).astype(BF16)
    cmat = jnp.concatenate([c_block(c_re), -c_block(c_im)], axis=0).astype(BF16)
    return bmat, cmat, abr.reshape(1, g * n), abi.reshape(1, g * n), d_skip.reshape(1, g * p)


def _tri_lower():
    j = np.arange(K_BLK)
    return jnp.asarray((j[None, :] <= j[:, None]).astype(np.float32), BF16)


def _stacked_eye():
    return jnp.asarray(np.tile(np.eye(Q_BLK, dtype=np.float32), (DSA_HEADS, 1)), BF16)


def _far_bias_cols(rel_bias):
    c = rel_bias[REL_BUCKETS - 1, :].astype(F32)
    hi = c.astype(BF16)
    lo = (c - hi.astype(F32)).astype(BF16)
    cols = jnp.zeros((DSA_HEADS, Q_BLK, DSA_HEAD_DIM), BF16)
    cols = cols.at[:, :, 0].set(hi[:, None])
    return cols.at[:, :, 1].set(lo[:, None])


def kernel(x, w_in, w_out, norm_mix, ssm_a_re, ssm_a_im, ssm_log_dt, ssm_b_re, ssm_b_im,
           ssm_c_re, ssm_c_im, ssm_d, ssm_glu_w, ssm_glu_b, rel_bias, ret_gn, norm_ffn,
           w_ffn_in, w_ffn_out, norm_final):
    bsz, seq, _ = x.shape
    depth = w_in.shape[0]
    w_all = _pack_w_in(w_in)
    wo = w_out.astype(BF16)
    wg = w_ffn_in[:, :, :D_FF].reshape(depth, D_MODEL, N_FF_CHUNK, FF_CHUNK)
    wu = w_ffn_in[:, :, D_FF:].reshape(depth, D_MODEL, N_FF_CHUNK, FF_CHUNK)
    wg = jnp.swapaxes(wg, 1, 2).astype(BF16)
    wu = jnp.swapaxes(wu, 1, 2).astype(BF16)
    wd = w_ffn_out.reshape(depth, N_FF_CHUNK, FF_CHUNK, D_MODEL).astype(BF16)
    glu_w = ssm_glu_w.astype(BF16)

    bias = _bias_tiles(rel_bias)
    tri = _tri_lower()
    eye = _stacked_eye()
    cfar = _far_bias_cols(rel_bias)
    tables = _ret_tables(seq, min(RET_CHUNK, seq))

    for l in range(depth):
        main, idxf, u_t = _inproj(x, norm_mix[l][None, :], w_all[l])
        s5p = _s5_params(ssm_a_re[l], ssm_a_im[l], ssm_log_dt[l], ssm_b_re[l], ssm_b_im[l],
                         ssm_c_re[l], ssm_c_im[l], ssm_d[l])
        ya = _s5(u_t.reshape(seq, bsz, S5_WIDTH), *s5p, glu_w[l], ssm_glu_b[l][None, :])
        yb = _dsa(main, idxf, bias, tri, eye, cfar)
        yc = _retention(main, tables, ret_gn[l][None, :])
        x = _ffn(x, ya.reshape(seq, bsz * S5_WIDTH), yb, yc, wo[l], norm_ffn[l][None, :],
                 wg[l], wu[l], wd[l], norm_final[None, :], final=(l == depth - 1))
    return x
```

```python
import functools
import math

import numpy as np
import jax
import jax.numpy as jnp
from jax import lax
from jax.experimental import pallas as pl
from jax.experimental.pallas import tpu as pltpu

F32 = jnp.float32
BF16 = jnp.bfloat16
I32 = jnp.int32
I16 = jnp.int16

D_MODEL = 1024
DEPTH = 4
S5_WIDTH = 256
S5_GROUP = 16
S5_GROUPS = 16
S5_STATE = 64
DSA_HEADS = 8
DSA_HEAD_DIM = 64
DSA_WIDTH = 512
DSA_KV_DIM = 64
IDX_HEADS = 4
IDX_DIM = 64
DSA_TOPK_MAX = 256
RET_HEADS = 4
RET_HEAD_DIM = 64
RET_WIDTH = 256
ROPE_BASE = 10000.0
REL_BUCKETS = 32
REL_MAX_DIST = 128
D_FF = 2816
IN_SPLITS = (256, 512, 64, 64, 256, 64, 4, 256, 256, 256, 256)
RMS_EPS = 1e-6
GN_EPS = 1e-6

COL_DQ = 0
COL_IQ = 512
COL_RQ = 768
COL_RK = 1024
COL_RV = 1280
COL_RG = 1536
COL_KG = 1792
COL_VG = 1920
COL_IX = 2048
MAIN_W = 2304
ONES_COLS = (COL_KG + 64, COL_KG + 65, COL_VG + 64)
PROJ_W = MAIN_W + S5_WIDTH
PROJ_CHUNK = 256

LANES = 128
Q_BLK = 128
Q_PAIR = 2
K_BLK = 256
FAR_UNROLL = 4
FAR_JOINT = 2
SCORE_UNROLL = 4
COUNT_UNROLL = 4
NEG_BIG = -1e30
INT_MIN = -2147483648
HALF16 = 32768
KEY_NEG_INF = -2139095041
FF_CHUNK = 256
N_FF_CHUNK = D_FF // FF_CHUNK
RET_CHUNK = 512
S5_TL = 256
TM_PROJ = 1024
TM_FFN = 1024
VMEM_LIMIT = 56 * 1024 * 1024


def _nt_dot(a, b):
    return lax.dot_general(a, b, (((1,), (1,)), ((), ())), preferred_element_type=F32)


def _dot(a, b):
    return jnp.dot(a, b, preferred_element_type=F32)


def _inproj_kernel(x_ref, g_ref, w_ref, main_ref, idx_ref, u_ref):
    x = x_ref[0]
    ms = jnp.mean(x * x, axis=-1, keepdims=True)
    h = (x * lax.rsqrt(ms + RMS_EPS) * g_ref[...]).astype(BF16)
    n_main = MAIN_W // PROJ_CHUNK
    for c in range(n_main + 1):
        acc = _dot(h, w_ref[:, c * PROJ_CHUNK:(c + 1) * PROJ_CHUNK])
        if c == n_main:
            u_ref[...] = acc
            continue
        for one_col in ONES_COLS:
            if c * PROJ_CHUNK <= one_col < (c + 1) * PROJ_CHUNK:
                lane = lax.broadcasted_iota(I32, acc.shape, 1)
                acc = jnp.where(lane == one_col - c * PROJ_CHUNK, 1.0, acc)
        if c * PROJ_CHUNK == COL_IX:
            idx_ref[0] = acc[:, :LANES]
        main_ref[0, :, c * PROJ_CHUNK:(c + 1) * PROJ_CHUNK] = acc.astype(BF16)


def _inproj(x, g, w_all):
    bsz, seq, _ = x.shape
    tm = min(TM_PROJ, seq)
    return pl.pallas_call(
        _inproj_kernel,
        grid=(bsz, seq // tm),
        in_specs=[
            pl.BlockSpec((1, tm, D_MODEL), lambda b, i: (b, i, 0)),
            pl.BlockSpec((1, D_MODEL), lambda b, i: (0, 0)),
            pl.BlockSpec((D_MODEL, PROJ_W), lambda b, i: (0, 0)),
        ],
        out_specs=[
            pl.BlockSpec((1, tm, MAIN_W), lambda b, i: (b, i, 0)),
            pl.BlockSpec((1, tm, LANES), lambda b, i: (b, i, 0)),
            pl.BlockSpec((tm, S5_WIDTH), lambda b, i: (i, b)),
        ],
        out_shape=[
            jax.ShapeDtypeStruct((bsz, seq, MAIN_W), BF16),
            jax.ShapeDtypeStruct((bsz, seq, LANES), F32),
            jax.ShapeDtypeStruct((seq, bsz * S5_WIDTH), F32),
        ],
        compiler_params=pltpu.CompilerParams(
            dimension_semantics=("arbitrary", "arbitrary"), vmem_limit_bytes=VMEM_LIMIT),
        name="inproj",
    )(x, g, w_all)


def _s5_kernel(u_ref, bmat_ref, cmat_ref, ar_ref, ai_ref, d_ref, gw_ref, gb_ref, o_ref,
               xs_ref, h_ref):
    tl, bsz, _ = u_ref.shape
    nst = S5_GROUPS * S5_STATE

    @pl.when(pl.program_id(0) == 0)
    def _():
        h_ref[...] = jnp.zeros_like(h_ref)

    u = u_ref[...].reshape(tl * bsz, S5_WIDTH)
    xs_ref[...] = _dot(u.astype(BF16), bmat_ref[...])
    ar = jnp.broadcast_to(ar_ref[...], (bsz, nst))
    ai = jnp.broadcast_to(ai_ref[...], (bsz, nst))

    def step(t, carry):
        hr, hi = carry
        rows = pl.ds(pl.multiple_of(t * bsz, bsz), bsz)
        nr = ar * hr - ai * hi + xs_ref[rows, 0:nst]
        ni = ar * hi + ai * hr + xs_ref[rows, nst:2 * nst]
        xs_ref[rows, 0:nst] = nr
        xs_ref[rows, nst:2 * nst] = ni
        return nr, ni

    hr, hi = lax.fori_loop(0, tl, step, (h_ref[:, 0:nst], h_ref[:, nst:2 * nst]), unroll=8)
    h_ref[:, 0:nst] = hr
    h_ref[:, nst:2 * nst] = hi

    y = _dot(xs_ref[...].astype(BF16), cmat_ref[...]) + d_ref[...] * u
    y = jax.nn.gelu(y)
    z = _dot(y.astype(BF16), gw_ref[...]) + gb_ref[...]
    o_ref[...] = (y * jax.nn.sigmoid(z)).reshape(tl, bsz, S5_WIDTH)


def _s5(u_t, bmat, cmat, ar, ai, dvec, gw, gb):
    seq, bsz, _ = u_t.shape
    tl = min(S5_TL, seq)
    nst2 = 2 * S5_GROUPS * S5_STATE
    const = lambda shape: pl.BlockSpec(shape, lambda i: (0,) * len(shape))
    return pl.pallas_call(
        _s5_kernel,
        grid=(seq // tl,),
        in_specs=[
            pl.BlockSpec((tl, bsz, S5_WIDTH), lambda i: (i, 0, 0)),
            const((S5_WIDTH, nst2)), const((nst2, S5_WIDTH)),
            const((1, nst2 // 2)), const((1, nst2 // 2)), const((1, S5_WIDTH)),
            const((S5_WIDTH, S5_WIDTH)), const((1, S5_WIDTH)),
        ],
        out_specs=pl.BlockSpec((tl, bsz, S5_WIDTH), lambda i: (i, 0, 0)),
        out_shape=jax.ShapeDtypeStruct((seq, bsz, S5_WIDTH), F32),
        scratch_shapes=[pltpu.VMEM((tl * bsz, nst2), F32), pltpu.VMEM((bsz, nst2), F32)],
        compiler_params=pltpu.CompilerParams(
            dimension_semantics=("arbitrary",), vmem_limit_bytes=VMEM_LIMIT),
        name="s5_scan",
    )(u_t, bmat, cmat, ar, ai, dvec, gw, gb)


def _bias_kernel(rb_ref, bk_ref, o_ref):
    h = pl.program_id(1)
    bk = bk_ref[0]
    acc = jnp.zeros(bk.shape, F32)
    for b in range(REL_BUCKETS):
        acc = jnp.where(bk == b, rb_ref[b, h], acc)
    o_ref[0] = acc - rb_ref[REL_BUCKETS - 1, h]


def _bucket_tiles():
    d = np.arange(4, dtype=np.int64)[:, None, None] * Q_BLK
    i = np.arange(Q_BLK, dtype=np.int64)[None, :, None]
    j = np.arange(K_BLK, dtype=np.int64)[None, None, :]
    n = np.maximum(d + i - j, 0)
    max_exact = REL_BUCKETS // 2
    nf = np.maximum(n, 1).astype(np.float32)
    large = max_exact + (np.log(nf / np.float32(max_exact))
                         / np.float32(math.log(REL_MAX_DIST / max_exact))
                         * np.float32(REL_BUCKETS - max_exact)).astype(np.int32)
    large = np.minimum(large, REL_BUCKETS - 1)
    return np.where(n < max_exact, n, large).astype(np.int32)


def _bias_tiles(rel_bias):
    buckets = jnp.asarray(_bucket_tiles())
    return pl.pallas_call(
        _bias_kernel,
        grid=(4, DSA_HEADS),
        in_specs=[
            pl.BlockSpec(memory_space=pltpu.SMEM),
            pl.BlockSpec((1, Q_BLK, K_BLK), lambda d, h: (d, 0, 0)),
        ],
        out_specs=pl.BlockSpec((1, Q_BLK, K_BLK), lambda d, h: (d, h, 0)),
        out_shape=jax.ShapeDtypeStruct((4, DSA_HEADS * Q_BLK, K_BLK), F32),
        name="rel_bias_tiles",
    )(rel_bias, buckets)


def _dsa_kernel(dq_ref, iq_ref, iw_ref, ix_ref, kg_ref, vg_ref, bias_ref, tri_ref, eye_ref,
                cfar_ref, y_ref, keys_s, kh_s, kl_s, qs_s, iqs_s, t_s, r_s, m_s, acc_s, er_s,
                *, k_sel):
    qp = pl.program_id(1)
    nkb = qp + 1
    sub = K_BLK // 8
    rows = lambda p: slice(p * Q_BLK, (p + 1) * Q_BLK)
    hrows = lambda h: slice(h * Q_BLK, (h + 1) * Q_BLK)

    w_rows = []
    for p in range(Q_PAIR):
        qs_s[p, :, 0:Q_BLK] = eye_ref[...]
        for h in range(DSA_HEADS):
            dq = dq_ref[0, rows(p), h * DSA_HEAD_DIM:(h + 1) * DSA_HEAD_DIM]
            qs_s[p, hrows(h), Q_BLK:] = jnp.concatenate(
                [dq * jnp.asarray(0.125, BF16), cfar_ref[h]], axis=1)
        for h in range(IDX_HEADS):
            iqs_s[p * IDX_HEADS * Q_BLK + h * Q_BLK:p * IDX_HEADS * Q_BLK + (h + 1) * Q_BLK, :] = (
                iq_ref[0, rows(p), h * IDX_DIM:(h + 1) * IDX_DIM])
        w_t = iw_ref[0, rows(p), :].T
        w_rows.append(jnp.concatenate(
            [w_t[IDX_DIM + h:IDX_DIM + h + 1, :] for h in range(IDX_HEADS)], axis=1) * 0.0625)
    w_row = jnp.concatenate(w_rows, axis=1)

    key_pos = lax.broadcasted_iota(I32, (K_BLK, Q_BLK), 0)
    lane_pos = lax.broadcasted_iota(I32, (K_BLK, Q_BLK), 1)

    def score_block(kb, carry):
        ks = pl.multiple_of(kb * K_BLK, K_BLK)
        ik = ix_ref[0, pl.ds(ks, K_BLK), 0:IDX_DIM]
        r = jnp.maximum(_nt_dot(ik, iqs_s[...]), 0.0) * w_row
        for p in range(Q_PAIR):
            c0 = p * IDX_HEADS * Q_BLK
            s = ((r[:, c0:c0 + Q_BLK] + r[:, c0 + Q_BLK:c0 + 2 * Q_BLK])
                 + (r[:, c0 + 2 * Q_BLK:c0 + 3 * Q_BLK] + r[:, c0 + 3 * Q_BLK:c0 + 4 * Q_BLK]))
            s = jnp.where(s == 0.0, 0.0, s)
            qry_pos = (Q_PAIR * qp + p) * Q_BLK + lane_pos
            s = jnp.where(key_pos + ks <= qry_pos, s, -jnp.inf)
            bits = lax.bitcast_convert_type(s, I32)
            key = bits ^ ((bits >> 31) & 0x7FFFFFFF)
            keys_s[p, pl.ds(ks, K_BLK), :] = key
            kh_s[p, pl.ds(ks, K_BLK), :] = (key >> 16).astype(I16)
            kl_s[p, pl.ds(ks, K_BLK), :] = ((key & 0xFFFF) - HALF16).astype(I16)
        return carry

    def score_group(j, carry):
        for u in range(SCORE_UNROLL):
            score_block(SCORE_UNROLL * j + u, carry)
        return carry

    lax.fori_loop(0, nkb // SCORE_UNROLL, score_group, 0)
    lax.fori_loop(SCORE_UNROLL * (nkb // SCORE_UNROLL), nkb, score_block, 0)

    t_s[...] = jnp.full(t_s.shape, KEY_NEG_INF, I32)
    r_s[...] = jnp.zeros_like(r_s)

    @pl.when(qp * Q_PAIR * Q_BLK >= k_sel)
    def _():
        sub16 = K_BLK // 16

        def count16(ref, cand):
            c16 = jnp.concatenate([cand, cand], axis=1).astype(I16)[:, None, None]
            def body(kb, acc):
                ks = pl.multiple_of(kb * K_BLK, K_BLK)
                blk = ref[:, pl.ds(ks, K_BLK), :].reshape(Q_PAIR, sub16 // 4, 4, 16, Q_BLK)
                one = jnp.ones(blk.shape, I16)
                hit = jnp.where(blk >= c16, one, jnp.zeros_like(one))
                for j in range(sub16 // 4):
                    acc = acc + hit[:, j]
                return acc
            def group(j, acc):
                for u in range(COUNT_UNROLL):
                    acc = body(COUNT_UNROLL * j + u, acc)
                return acc
            acc = lax.fori_loop(0, nkb // COUNT_UNROLL, group,
                                jnp.zeros((Q_PAIR, 4, 16, Q_BLK), I16))
            acc = lax.fori_loop(COUNT_UNROLL * (nkb // COUNT_UNROLL), nkb, body, acc)
            tot = jnp.sum(jnp.sum(acc.astype(I32), axis=1), axis=1, keepdims=True)
            return jnp.broadcast_to(tot, (Q_PAIR, 8, Q_BLK))

        def bisect16(ref, target):
            def bit_step(i, tu):
                cand_u = tu | lax.shift_left(jnp.int32(1), 15 - i)
                return jnp.where(count16(ref, cand_u - HALF16) >= target, cand_u, tu)
            return lax.fori_loop(0, 16, bit_step, jnp.zeros((Q_PAIR, 8, Q_BLK), I32))

        hi = bisect16(kh_s, k_sel) - HALF16
        need = k_sel - count16(kh_s, hi + 1)
        hi16 = jnp.concatenate([hi, hi], axis=1).astype(I16)

        def keep_equal_high(kb, carry):
            ks = pl.multiple_of(kb * K_BLK, K_BLK)
            same = kh_s[:, pl.ds(ks, K_BLK), :].reshape(Q_PAIR, sub16, 16, Q_BLK) == hi16[:, None]
            low = kl_s[:, pl.ds(ks, K_BLK), :].reshape(Q_PAIR, sub16, 16, Q_BLK)
            kl_s[:, pl.ds(ks, K_BLK), :] = jnp.where(
                same, low, jnp.full(low.shape, -HALF16, I16)).reshape(Q_PAIR, K_BLK, Q_BLK)
            return carry

        lax.fori_loop(0, nkb, keep_equal_high, 0)
        tk = hi * (2 * HALF16) + bisect16(kl_s, need)

        def count_gt(kb, acc):
            ks = pl.multiple_of(kb * K_BLK, K_BLK)
            blk = keys_s[:, pl.ds(ks, K_BLK), :].reshape(Q_PAIR, sub // 4, 4, 8, Q_BLK)
            return acc + jnp.sum(jnp.where(blk > tk[:, None, None], 1.0, 0.0), axis=1)

        acc = lax.fori_loop(0, nkb, count_gt, jnp.zeros((Q_PAIR, 4, 8, Q_BLK), F32))
        t_s[...] = tk
        r_s[...] = k_sel - jnp.broadcast_to(
            jnp.sum(jnp.sum(acc, axis=1), axis=1, keepdims=True), (Q_PAIR, 8, Q_BLK))

    m_s[...] = jnp.full(m_s.shape, NEG_BIG, F32)
    acc_s[...] = jnp.zeros_like(acc_s)
    er_s[...] = jnp.zeros_like(er_s)

    def logits(kb, p, near):
        ks = pl.multiple_of(kb * K_BLK, K_BLK)
        t8, r8 = t_s[p], r_s[p]
        key = keys_s[p, pl.ds(ks, K_BLK), :].reshape(sub, 8, Q_BLK)
        eq = key == t8
        pre = _dot(tri_ref[...], jnp.where(eq, 1.0, 0.0).reshape(K_BLK, Q_BLK).astype(BF16))
        er = er_s[p]
        take = pre.reshape(sub, 8, Q_BLK) + er <= r8
        pen_t = jnp.where(key > t8, 0.0, jnp.where(eq, jnp.where(take, 0.0, NEG_BIG), NEG_BIG))
        er_s[p] = er + jnp.broadcast_to(pre[K_BLK - 1:K_BLK, :], (8, Q_BLK))
        rhs = jnp.concatenate(
            [pen_t.reshape(K_BLK, Q_BLK).astype(BF16), kg_ref[0, pl.ds(ks, K_BLK), :]], axis=1)
        s = _nt_dot(qs_s[p], rhs)
        if near:
            s = s + bias_ref[jnp.minimum(Q_PAIR * qp + p - 2 * kb, 3)]
        return s

    def attend(kbs, near):
        for p in range(Q_PAIR):
            ss = [logits(kb, p, near) for kb in kbs]
            m_old = m_s[p]
            s_max = functools.reduce(jnp.maximum, ss)
            m_new = jnp.maximum(m_old, jnp.max(s_max, axis=1, keepdims=True))
            m2 = jnp.concatenate([m_new, m_new], axis=1)
            pr = jnp.concatenate([jnp.exp(s - m2).astype(BF16) for s in ss], axis=1)
            vg = jnp.concatenate(
                [vg_ref[0, pl.ds(pl.multiple_of(kb * K_BLK, K_BLK), K_BLK), :] for kb in kbs],
                axis=0)
            acc_s[p] = jnp.exp(m_old - m_new) * acc_s[p] + _dot(pr, vg)
            m_s[p] = m_new

    n_far = jnp.maximum(nkb - 2, 0)
    n_grp = n_far // FAR_UNROLL

    def far_group(j, carry):
        for u in range(0, FAR_UNROLL, FAR_JOINT):
            attend([FAR_UNROLL * j + u + v for v in range(FAR_JOINT)], near=False)
        return carry

    def far_single(kb, carry):
        attend([kb], near=False)
        return carry

    lax.fori_loop(0, n_grp, far_group, 0)
    lax.fori_loop(FAR_UNROLL * n_grp, n_far, far_single, 0)

    @pl.when(nkb >= 2)
    def _():
        attend([nkb - 2, nkb - 1], near=True)

    @pl.when(nkb < 2)
    def _():
        attend([nkb - 1], near=True)

    for p in range(Q_PAIR):
        acc = acc_s[p]
        o = acc[:, 0:DSA_HEAD_DIM] / acc[:, DSA_HEAD_DIM:DSA_HEAD_DIM + 1]
        for h in range(DSA_HEADS):
            y_ref[0, rows(p), h * DSA_HEAD_DIM:(h + 1) * DSA_HEAD_DIM] = o[hrows(h)].astype(BF16)


def _dsa(main, idxf, bias, tri, eye, cfar):
    bsz, seq, _ = main.shape
    k_sel = min(DSA_TOPK_MAX, seq // 4)
    qrows = Q_PAIR * Q_BLK
    assert qrows == K_BLK and seq % K_BLK == 0 and k_sel % qrows == 0
    col = lambda c, w: c // w
    return pl.pallas_call(
        functools.partial(_dsa_kernel, k_sel=k_sel),
        grid=(bsz, seq // qrows),
        in_specs=[
            pl.BlockSpec((1, qrows, DSA_WIDTH), lambda b, q: (b, q, col(COL_DQ, DSA_WIDTH))),
            pl.BlockSpec((1, qrows, 256), lambda b, q: (b, q, col(COL_IQ, 256))),
            pl.BlockSpec((1, qrows, LANES), lambda b, q: (b, q, 0)),
            pl.BlockSpec((1, seq, LANES), lambda b, q: (b, 0, col(COL_IX, LANES))),
            pl.BlockSpec((1, seq, LANES), lambda b, q: (b, 0, col(COL_KG, LANES))),
            pl.BlockSpec((1, seq, LANES), lambda b, q: (b, 0, col(COL_VG, LANES))),
            pl.BlockSpec((4, DSA_HEADS * Q_BLK, K_BLK), lambda b, q: (0, 0, 0)),
            pl.BlockSpec((K_BLK, K_BLK), lambda b, q: (0, 0)),
            pl.BlockSpec((DSA_HEADS * Q_BLK, Q_BLK), lambda b, q: (0, 0)),
            pl.BlockSpec((DSA_HEADS, Q_BLK, DSA_HEAD_DIM), lambda b, q: (0, 0, 0)),
        ],
        out_specs=pl.BlockSpec((1, qrows, DSA_WIDTH), lambda b, q: (b, q, 0)),
        out_shape=jax.ShapeDtypeStruct((bsz, seq, DSA_WIDTH), BF16),
        scratch_shapes=[
            pltpu.VMEM((Q_PAIR, seq, Q_BLK), I32),
            pltpu.VMEM((Q_PAIR, seq, Q_BLK), I16),
            pltpu.VMEM((Q_PAIR, seq, Q_BLK), I16),
            pltpu.VMEM((Q_PAIR, DSA_HEADS * Q_BLK, 2 * Q_BLK), BF16),
            pltpu.VMEM((Q_PAIR * IDX_HEADS * Q_BLK, IDX_DIM), BF16),
            pltpu.VMEM((Q_PAIR, 8, Q_BLK), I32),
            pltpu.VMEM((Q_PAIR, 8, Q_BLK), F32),
            pltpu.VMEM((Q_PAIR, DSA_HEADS * Q_BLK, LANES), F32),
            pltpu.VMEM((Q_PAIR, DSA_HEADS * Q_BLK, LANES), F32),
            pltpu.VMEM((Q_PAIR, 8, Q_BLK), F32),
        ],
        compiler_params=pltpu.CompilerParams(
            dimension_semantics=("arbitrary", "arbitrary"), vmem_limit_bytes=VMEM_LIMIT),
        name="dsa_attention",
    )(main, main, idxf, main, main, main, bias, tri, eye, cfar)


def _ret_kernel(q_ref, k_ref, v_ref, g_ref, cos_ref, sa_ref, sb_ref, dec_ref, wq_ref, wk_ref,
                gc_ref, gn_ref, o_ref, st_ref):
    @pl.when(pl.program_id(1) == 0)
    def _():
        st_ref[...] = jnp.zeros_like(st_ref)

    cos, sa, sb = cos_ref[...], sa_ref[...], sb_ref[...]
    half = RET_HEAD_DIM // 2

    def rot(x):
        return (x * cos + pltpu.roll(x, RET_WIDTH - half, 1) * sa + pltpu.roll(x, half, 1) * sb)

    q = rot(q_ref[0].astype(F32))
    k = rot(k_ref[0].astype(F32)) * (RET_HEAD_DIM ** -0.5)
    v = v_ref[0]
    g = g_ref[0].astype(F32)
    gate = g * jax.nn.sigmoid(g)
    gn = gn_ref[...]
    for h in range(RET_HEADS):
        sl = slice(h * RET_HEAD_DIM, (h + 1) * RET_HEAD_DIM)
        qh = q[:, sl].astype(BF16)
        kf = k[:, sl]
        vh = v[:, sl]
        s = _nt_dot(qh, kf.astype(BF16)) * dec_ref[h]
        st = st_ref[h]
        o = _dot(s.astype(BF16), vh) + _dot(qh, st.astype(BF16)) * wq_ref[h]
        kw_t = (kf * wk_ref[h]).T.astype(BF16)
        st_ref[h] = gc_ref[h] * st + _dot(kw_t, vh)
        mu = jnp.mean(o, axis=-1, keepdims=True)
        d = o - mu
        var = jnp.mean(d * d, axis=-1, keepdims=True)
        on = d * lax.rsqrt(var + GN_EPS)
        o_ref[0, :, sl] = (gate[:, sl] * (on * gn[:, sl])).astype(BF16)


def _ret_tables(seq, chunk):
    half = RET_HEAD_DIM // 2
    inv = ROPE_BASE ** (-jnp.arange(half, dtype=F32) / half)
    ang = jnp.arange(seq, dtype=jnp.int32).astype(F32)[:, None] * inv[None, :]
    cos, sin = jnp.cos(ang), jnp.sin(ang)
    zero = jnp.zeros_like(sin)
    cos_t = jnp.tile(cos, (1, 2 * RET_HEADS))
    sa = jnp.tile(jnp.concatenate([-sin, zero], axis=1), (1, RET_HEADS))
    sb = jnp.tile(jnp.concatenate([zero, sin], axis=1), (1, RET_HEADS))
    gamma = 1.0 - 2.0 ** (-5.0 - jnp.arange(RET_HEADS, dtype=F32))
    log_g = jnp.log(gamma)
    i = jnp.arange(chunk, dtype=F32)
    diff = i[:, None] - i[None, :]
    decay = jnp.where(diff >= 0, jnp.exp(log_g[:, None, None] * jnp.maximum(diff, 0.0)), 0.0)
    ones = jnp.ones((1, 1, RET_HEAD_DIM), F32)
    wq = jnp.exp(log_g[:, None] * (i + 1.0)[None, :])[:, :, None] * ones
    wk = jnp.exp(log_g[:, None] * (chunk - 1.0 - i)[None, :])[:, :, None] * ones
    gc = jnp.exp(log_g * chunk)[:, None, None] * jnp.ones((1, RET_HEAD_DIM, RET_HEAD_DIM), F32)
    return cos_t, sa, sb, decay, wq, wk, gc


def _retention(main, tables, gn):
    bsz, seq, _ = main.shape
    cos_t, sa, sb, decay, wq, wk, gc = tables
    c = decay.shape[-1]
    blk = lambda col: pl.BlockSpec((1, c, RET_WIDTH), lambda b, i: (b, i, col // RET_WIDTH))
    tab = pl.BlockSpec((c, RET_WIDTH), lambda b, i: (i, 0))
    const = lambda shape: pl.BlockSpec(shape, lambda b, i: (0,) * len(shape))
    return pl.pallas_call(
        _ret_kernel,
        grid=(bsz, seq // c),
        in_specs=[
            blk(COL_RQ), blk(COL_RK), blk(COL_RV), blk(COL_RG), tab, tab, tab,
            const((RET_HEADS, c, c)), const((RET_HEADS, c, RET_HEAD_DIM)),
            const((RET_HEADS, c, RET_HEAD_DIM)),
            const((RET_HEADS, RET_HEAD_DIM, RET_HEAD_DIM)), const((1, RET_WIDTH)),
        ],
        out_specs=pl.BlockSpec((1, c, RET_WIDTH), lambda b, i: (b, i, 0)),
        out_shape=jax.ShapeDtypeStruct((bsz, seq, RET_WIDTH), BF16),
        scratch_shapes=[pltpu.VMEM((RET_HEADS, RET_HEAD_DIM, RET_HEAD_DIM), F32)],
        compiler_params=pltpu.CompilerParams(
            dimension_semantics=("arbitrary", "arbitrary"), vmem_limit_bytes=VMEM_LIMIT),
        name="retention",
    )(main, main, main, main, cos_t, sa, sb, decay, wq, wk, gc, gn)


def _ffn_kernel(x_ref, ya_ref, yb_ref, yc_ref, wo_ref, g_ref, wg_ref, wu_ref, wd_ref, gf_ref,
                o_ref, x1_s, h_s, acc_s, *, final):
    x1 = (x_ref[0]
          + _dot(ya_ref[...].astype(BF16), wo_ref[0:S5_WIDTH, :])
          + _dot(yb_ref[0], wo_ref[S5_WIDTH:S5_WIDTH + DSA_WIDTH, :])
          + _dot(yc_ref[0], wo_ref[S5_WIDTH + DSA_WIDTH:, :]))
    x1_s[...] = x1
    ms = jnp.mean(x1 * x1, axis=-1, keepdims=True)
    h_s[...] = (x1 * lax.rsqrt(ms + RMS_EPS) * g_ref[...]).astype(BF16)
    acc_s[...] = jnp.zeros_like(acc_s)

    for c in range(N_FF_CHUNK):
        cols = slice(c * FF_CHUNK, (c + 1) * FF_CHUNK)
        h = h_s[...]
        gate = _dot(h, wg_ref[:, cols])
        up = _dot(h, wu_ref[:, cols])
        a = (gate * jax.nn.sigmoid(gate) * up).astype(BF16)
        acc_s[...] += _dot(a, wd_ref[cols, :])
    y = x1_s[...] + acc_s[...]
    if final:
        ms = jnp.mean(y * y, axis=-1, keepdims=True)
        y = y * lax.rsqrt(ms + RMS_EPS) * gf_ref[...]
    o_ref[0] = y


def _ffn(x, ya, yb, yc, wo, g, wg, wu, wd, gf, final):
    bsz, seq, _ = x.shape
    tm = min(TM_FFN, seq)
    once = pl.Buffered(1)
    const = lambda shape: pl.BlockSpec(shape, lambda b, i: (0,) * len(shape), pipeline_mode=once)
    return pl.pallas_call(
        functools.partial(_ffn_kernel, final=final),
        grid=(bsz, seq // tm),
        in_specs=[
            pl.BlockSpec((1, tm, D_MODEL), lambda b, i: (b, i, 0)),
            pl.BlockSpec((tm, S5_WIDTH), lambda b, i: (i, b)),
            pl.BlockSpec((1, tm, DSA_WIDTH), lambda b, i: (b, i, 0)),
            pl.BlockSpec((1, tm, RET_WIDTH), lambda b, i: (b, i, 0)),
            const((D_MODEL, D_MODEL)), const((1, D_MODEL)),
            const((D_MODEL, D_FF)), const((D_MODEL, D_FF)),
            const((D_FF, D_MODEL)), const((1, D_MODEL)),
        ],
        out_specs=pl.BlockSpec((1, tm, D_MODEL), lambda b, i: (b, i, 0)),
        out_shape=jax.ShapeDtypeStruct((bsz, seq, D_MODEL), F32),
        scratch_shapes=[pltpu.VMEM((tm, D_MODEL), F32), pltpu.VMEM((tm, D_MODEL), BF16),
                        pltpu.VMEM((tm, D_MODEL), F32)],
        compiler_params=pltpu.CompilerParams(
            dimension_semantics=("arbitrary", "arbitrary"), vmem_limit_bytes=VMEM_LIMIT),
        name="outproj_ffn",
    )(x, ya, yb, yc, wo, g, wg, wu, wd, gf)


def _pack_w_in(w_in):
    pts = [int(s) for s in np.cumsum(IN_SPLITS)[:-1]]
    u, dq, dk, dv, iq, ik, iw, rq, rk, rv, rg = jnp.split(w_in, pts, axis=-1)
    z = lambda n: jnp.zeros(w_in.shape[:-1] + (n,), w_in.dtype)
    parts = [dq, iq, rq, rk, rv, rg, dk, z(64), dv, z(64), ik, iw, z(60), z(128), u]
    return jnp.concatenate(parts, axis=-1).astype(BF16)


def _s5_params(a_re, a_im, log_dt, b_re, b_im, c_re, c_im, d_skip):
    g, n, p = S5_GROUPS, S5_STATE, S5_GROUP
    dt = jnp.exp(log_dt)[:, None]
    mag = jnp.exp(a_re * dt)
    ang = a_im * dt
    abr, abi = mag * jnp.cos(ang), mag * jnp.sin(ang)
    den = a_re * a_re + a_im * a_im
    nr, ni = abr - 1.0, abi
    cr = (nr * a_re + ni * a_im) / den
    ci = (ni * a_re - nr * a_im) / den
    bbr = cr[..., None] * b_re - ci[..., None] * b_im
    bbi = cr[..., None] * b_im + ci[..., None] * b_re
    eye = jnp.eye(g, dtype=F32)

    def b_block(bb):
        return (jnp.swapaxes(bb, 1, 2)[:, :, None, :] * eye[:, None, :, None]).reshape(g * p, g * n)

    def c_block(cc):
        return (jnp.swapaxes(cc, 1, 2)[:, :, None, :] * eye[:, None, :, None]).reshape(g * n, g * p)

    bmat = jnp.concatenate([b_block(bbr), b_block(bbi)], axis=1).astype(BF16)
    cmat = jnp.concatenate([c_block(c_re), -c_block(c_im)], axis=0).astype(BF16)
    return bmat, cmat, abr.reshape(1, g * n), abi.reshape(1, g * n), d_skip.reshape(1, g * p)


def _tri_lower():
    j = np.arange(K_BLK)
    return jnp.asarray((j[None, :] <= j[:, None]).astype(np.float32), BF16)


def _stacked_eye():
    return jnp.asarray(np.tile(np.eye(Q_BLK, dtype=np.float32), (DSA_HEADS, 1)), BF16)


def _far_bias_cols(rel_bias):
    c = rel_bias[REL_BUCKETS - 1, :].astype(F32)
    hi = c.astype(BF16)
    lo = (c - hi.astype(F32)).astype(BF16)
    cols = jnp.zeros((DSA_HEADS, Q_BLK, DSA_HEAD_DIM), BF16)
    cols = cols.at[:, :, 0].set(hi[:, None])
    return cols.at[:, :, 1].set(lo[:, None])


def kernel(x, w_in, w_out, norm_mix, ssm_a_re, ssm_a_im, ssm_log_dt, ssm_b_re, ssm_b_im,
           ssm_c_re, ssm_c_im, ssm_d, ssm_glu_w, ssm_glu_b, rel_bias, ret_gn, norm_ffn,
           w_ffn_in, w_ffn_out, norm_final):
    bsz, seq, _ = x.shape
    depth = w_in.shape[0]
    w_all = _pack_w_in(w_in)
    wo = w_out.astype(BF16)
    wg = w_ffn_in[:, :, :D_FF].astype(BF16)
    wu = w_ffn_in[:, :, D_FF:].astype(BF16)
    wd = w_ffn_out.astype(BF16)
    glu_w = ssm_glu_w.astype(BF16)

    bias = _bias_tiles(rel_bias)
    tri = _tri_lower()
    eye = _stacked_eye()
    cfar = _far_bias_cols(rel_bias)
    tables = _ret_tables(seq, min(RET_CHUNK, seq))
    s5p_all = jax.vmap(_s5_params)(ssm_a_re, ssm_a_im, ssm_log_dt, ssm_b_re, ssm_b_im,
                                   ssm_c_re, ssm_c_im, ssm_d)

    for l in range(depth):
        main, idxf, u_t = _inproj(x, norm_mix[l][None, :], w_all[l])
        s5p = [a[l] for a in s5p_all]
        ya = _s5(u_t.reshape(seq, bsz, S5_WIDTH), *s5p, glu_w[l], ssm_glu_b[l][None, :])
        yb = _dsa(main, idxf, bias, tri, eye, cfar)
        yc = _retention(main, tables, ret_gn[l][None, :])
        x = _ffn(x, ya.reshape(seq, bsz * S5_WIDTH), yb, yc, wo[l], norm_ffn[l][None, :],
                 wg[l], wu[l], wd[l], norm_final[None, :], final=(l == depth - 1))
    return x
```

```python
import functools
import math

import numpy as np
import jax
import jax.numpy as jnp
from jax import lax
from jax.experimental import pallas as pl
from jax.experimental.pallas import tpu as pltpu

F32 = jnp.float32
BF16 = jnp.bfloat16
I32 = jnp.int32
I16 = jnp.int16

D_MODEL = 1024
DEPTH = 4
S5_WIDTH = 256
S5_GROUP = 16
S5_GROUPS = 16
S5_STATE = 64
DSA_HEADS = 8
DSA_HEAD_DIM = 64
DSA_WIDTH = 512
DSA_KV_DIM = 64
IDX_HEADS = 4
IDX_DIM = 64
DSA_TOPK_MAX = 256
RET_HEADS = 4
RET_HEAD_DIM = 64
RET_WIDTH = 256
ROPE_BASE = 10000.0
REL_BUCKETS = 32
REL_MAX_DIST = 128
D_FF = 2816
IN_SPLITS = (256, 512, 64, 64, 256, 64, 4, 256, 256, 256, 256)
RMS_EPS = 1e-6
GN_EPS = 1e-6

COL_DQ = 0
COL_IQ = 512
COL_RQ = 768
COL_RK = 1024
COL_RV = 1280
COL_RG = 1536
COL_KG = 1792
COL_VG = 1920
COL_IX = 2048
MAIN_W = 2304
ONES_COLS = (COL_KG + 64, COL_KG + 65, COL_VG + 64)
PROJ_W = MAIN_W + S5_WIDTH
PROJ_CHUNK = 256

LANES = 128
Q_BLK = 128
Q_PAIR = 2
K_BLK = 256
FAR_UNROLL = 4
FAR_JOINT = 2
SCORE_UNROLL = 4
COUNT_UNROLL = 4
NEG_BIG = -1e30
INT_MIN = -2147483648
HALF16 = 32768
KEY_NEG_INF = -2139095041
FF_CHUNK = 256
N_FF_CHUNK = D_FF // FF_CHUNK
RET_CHUNK = 512
S5_TL = 256
TM_PROJ = 1024
TM_FFN = 1024
VMEM_LIMIT = 56 * 1024 * 1024


def _nt_dot(a, b):
    return lax.dot_general(a, b, (((1,), (1,)), ((), ())), preferred_element_type=F32)


def _dot(a, b):
    return jnp.dot(a, b, preferred_element_type=F32)


def _inproj_kernel(x_ref, g_ref, w_ref, main_ref, idx_ref, u_ref):
    x = x_ref[0]
    ms = jnp.mean(x * x, axis=-1, keepdims=True)
    h = (x * lax.rsqrt(ms + RMS_EPS) * g_ref[...]).astype(BF16)
    n_main = MAIN_W // PROJ_CHUNK
    for c in range(n_main + 1):
        acc = _dot(h, w_ref[:, c * PROJ_CHUNK:(c + 1) * PROJ_CHUNK])
        if c == n_main:
            u_ref[...] = acc
            continue
        for one_col in ONES_COLS:
            if c * PROJ_CHUNK <= one_col < (c + 1) * PROJ_CHUNK:
                lane = lax.broadcasted_iota(I32, acc.shape, 1)
                acc = jnp.where(lane == one_col - c * PROJ_CHUNK, 1.0, acc)
        if c * PROJ_CHUNK == COL_IX:
            idx_ref[0] = acc[:, :LANES]
        main_ref[0, :, c * PROJ_CHUNK:(c + 1) * PROJ_CHUNK] = acc.astype(BF16)


def _inproj(x, g, w_all):
    bsz, seq, _ = x.shape
    tm = min(TM_PROJ, seq)
    return pl.pallas_call(
        _inproj_kernel,
        grid=(bsz, seq // tm),
        in_specs=[
            pl.BlockSpec((1, tm, D_MODEL), lambda b, i: (b, i, 0)),
            pl.BlockSpec((1, D_MODEL), lambda b, i: (0, 0)),
            pl.BlockSpec((D_MODEL, PROJ_W), lambda b, i: (0, 0)),
        ],
        out_specs=[
            pl.BlockSpec((1, tm, MAIN_W), lambda b, i: (b, i, 0)),
            pl.BlockSpec((1, tm, LANES), lambda b, i: (b, i, 0)),
            pl.BlockSpec((tm, S5_WIDTH), lambda b, i: (i, b)),
        ],
        out_shape=[
            jax.ShapeDtypeStruct((bsz, seq, MAIN_W), BF16),
            jax.ShapeDtypeStruct((bsz, seq, LANES), F32),
            jax.ShapeDtypeStruct((seq, bsz * S5_WIDTH), F32),
        ],
        compiler_params=pltpu.CompilerParams(
            dimension_semantics=("arbitrary", "arbitrary"), vmem_limit_bytes=VMEM_LIMIT),
        name="inproj",
    )(x, g, w_all)


def _s5_kernel(u_ref, bmat_ref, cmat_ref, ar_ref, ai_ref, d_ref, gw_ref, gb_ref, o_ref,
               xs_ref, h_ref):
    tl, bsz, _ = u_ref.shape
    nst = S5_GROUPS * S5_STATE

    @pl.when(pl.program_id(0) == 0)
    def _():
        h_ref[...] = jnp.zeros_like(h_ref)

    u = u_ref[...].reshape(tl * bsz, S5_WIDTH)
    xs_ref[...] = _dot(u.astype(BF16), bmat_ref[...])
    ar = jnp.broadcast_to(ar_ref[...], (bsz, nst))
    ai = jnp.broadcast_to(ai_ref[...], (bsz, nst))

    def step(t, carry):
        hr, hi = carry
        rows = pl.ds(pl.multiple_of(t * bsz, bsz), bsz)
        nr = ar * hr - ai * hi + xs_ref[rows, 0:nst]
        ni = ar * hi + ai * hr + xs_ref[rows, nst:2 * nst]
        xs_ref[rows, 0:nst] = nr
        xs_ref[rows, nst:2 * nst] = ni
        return nr, ni

    hr, hi = lax.fori_loop(0, tl, step, (h_ref[:, 0:nst], h_ref[:, nst:2 * nst]), unroll=8)
    h_ref[:, 0:nst] = hr
    h_ref[:, nst:2 * nst] = hi

    y = _dot(xs_ref[...].astype(BF16), cmat_ref[...]) + d_ref[...] * u
    y = jax.nn.gelu(y)
    z = _dot(y.astype(BF16), gw_ref[...]) + gb_ref[...]
    o_ref[...] = (y * jax.nn.sigmoid(z)).reshape(tl, bsz, S5_WIDTH)


def _s5(u_t, bmat, cmat, ar, ai, dvec, gw, gb):
    seq, bsz, _ = u_t.shape
    tl = min(S5_TL, seq)
    nst2 = 2 * S5_GROUPS * S5_STATE
    const = lambda shape: pl.BlockSpec(shape, lambda i: (0,) * len(shape))
    return pl.pallas_call(
        _s5_kernel,
        grid=(seq // tl,),
        in_specs=[
            pl.BlockSpec((tl, bsz, S5_WIDTH), lambda i: (i, 0, 0)),
            const((S5_WIDTH, nst2)), const((nst2, S5_WIDTH)),
            const((1, nst2 // 2)), const((1, nst2 // 2)), const((1, S5_WIDTH)),
            const((S5_WIDTH, S5_WIDTH)), const((1, S5_WIDTH)),
        ],
        out_specs=pl.BlockSpec((tl, bsz, S5_WIDTH), lambda i: (i, 0, 0)),
        out_shape=jax.ShapeDtypeStruct((seq, bsz, S5_WIDTH), F32),
        scratch_shapes=[pltpu.VMEM((tl * bsz, nst2), F32), pltpu.VMEM((bsz, nst2), F32)],
        compiler_params=pltpu.CompilerParams(
            dimension_semantics=("arbitrary",), vmem_limit_bytes=VMEM_LIMIT),
        name="s5_scan",
    )(u_t, bmat, cmat, ar, ai, dvec, gw, gb)


def _bias_kernel(rb_ref, bk_ref, o_ref):
    h = pl.program_id(1)
    bk = bk_ref[0]
    acc = jnp.zeros(bk.shape, F32)
    for b in range(REL_BUCKETS):
        acc = jnp.where(bk == b, rb_ref[b, h], acc)
    o_ref[0] = acc - rb_ref[REL_BUCKETS - 1, h]


def _bucket_tiles():
    d = np.arange(4, dtype=np.int64)[:, None, None] * Q_BLK
    i = np.arange(Q_BLK, dtype=np.int64)[None, :, None]
    j = np.arange(K_BLK, dtype=np.int64)[None, None, :]
    n = np.maximum(d + i - j, 0)
    max_exact = REL_BUCKETS // 2
    nf = np.maximum(n, 1).astype(np.float32)
    large = max_exact + (np.log(nf / np.float32(max_exact))
                         / np.float32(math.log(REL_MAX_DIST / max_exact))
                         * np.float32(REL_BUCKETS - max_exact)).astype(np.int32)
    large = np.minimum(large, REL_BUCKETS - 1)
    return np.where(n < max_exact, n, large).astype(np.int32)


def _bias_tiles(rel_bias):
    buckets = jnp.asarray(_bucket_tiles())
    return pl.pallas_call(
        _bias_kernel,
        grid=(4, DSA_HEADS),
        in_specs=[
            pl.BlockSpec(memory_space=pltpu.SMEM),
            pl.BlockSpec((1, Q_BLK, K_BLK), lambda d, h: (d, 0, 0)),
        ],
        out_specs=pl.BlockSpec((1, Q_BLK, K_BLK), lambda d, h: (d, h, 0)),
        out_shape=jax.ShapeDtypeStruct((4, DSA_HEADS * Q_BLK, K_BLK), F32),
        name="rel_bias_tiles",
    )(rel_bias, buckets)


def _dsa_kernel(dq_ref, iq_ref, iw_ref, ix_ref, kg_ref, vg_ref, bias_ref, tri_ref, eye_ref,
                cfar_ref, y_ref, keys_s, kh_s, kl_s, qs_s, iqs_s, t_s, r_s, m_s, acc_s, er_s,
                *, k_sel):
    qp = pl.program_id(1)
    nkb = qp + 1
    sub = K_BLK // 8
    rows = lambda p: slice(p * Q_BLK, (p + 1) * Q_BLK)
    hrows = lambda h: slice(h * Q_BLK, (h + 1) * Q_BLK)

    w_rows = []
    for p in range(Q_PAIR):
        @pl.when(qp == 0)
        def _():
            qs_s[p, :, 0:Q_BLK] = eye_ref[...]
        for h in range(DSA_HEADS):
            dq = dq_ref[0, rows(p), h * DSA_HEAD_DIM:(h + 1) * DSA_HEAD_DIM]
            qs_s[p, hrows(h), Q_BLK:] = jnp.concatenate(
                [dq * jnp.asarray(0.125, BF16), cfar_ref[h]], axis=1)
        for h in range(IDX_HEADS):
            iqs_s[p * IDX_HEADS * Q_BLK + h * Q_BLK:p * IDX_HEADS * Q_BLK + (h + 1) * Q_BLK, :] = (
                iq_ref[0, rows(p), h * IDX_DIM:(h + 1) * IDX_DIM])
        w_t = iw_ref[0, rows(p), :].T
        w_rows.append(jnp.concatenate(
            [w_t[IDX_DIM + h:IDX_DIM + h + 1, :] for h in range(IDX_HEADS)], axis=1) * 0.0625)
    w_row = jnp.concatenate(w_rows, axis=1)

    key_pos = lax.broadcasted_iota(I32, (K_BLK, Q_BLK), 0)
    lane_pos = lax.broadcasted_iota(I32, (K_BLK, Q_BLK), 1)

    def score_block(kb, carry):
        ks = pl.multiple_of(kb * K_BLK, K_BLK)
        ik = ix_ref[0, pl.ds(ks, K_BLK), 0:IDX_DIM]
        r = jnp.maximum(_nt_dot(ik, iqs_s[...]), 0.0) * w_row
        for p in range(Q_PAIR):
            c0 = p * IDX_HEADS * Q_BLK
            s = ((r[:, c0:c0 + Q_BLK] + r[:, c0 + Q_BLK:c0 + 2 * Q_BLK])
                 + (r[:, c0 + 2 * Q_BLK:c0 + 3 * Q_BLK] + r[:, c0 + 3 * Q_BLK:c0 + 4 * Q_BLK]))
            s = jnp.where(s == 0.0, 0.0, s)
            qry_pos = (Q_PAIR * qp + p) * Q_BLK + lane_pos
            s = jnp.where(key_pos + ks <= qry_pos, s, -jnp.inf)
            bits = lax.bitcast_convert_type(s, I32)
            key = bits ^ ((bits >> 31) & 0x7FFFFFFF)
            keys_s[p, pl.ds(ks, K_BLK), :] = key
            kh_s[p, pl.ds(ks, K_BLK), :] = (key >> 16).astype(I16)
            kl_s[p, pl.ds(ks, K_BLK), :] = ((key & 0xFFFF) - HALF16).astype(I16)
        return carry

    def score_group(j, carry):
        for u in range(SCORE_UNROLL):
            score_block(SCORE_UNROLL * j + u, carry)
        return carry

    lax.fori_loop(0, nkb // SCORE_UNROLL, score_group, 0)
    lax.fori_loop(SCORE_UNROLL * (nkb // SCORE_UNROLL), nkb, score_block, 0)

    t_s[...] = jnp.full(t_s.shape, KEY_NEG_INF, I32)
    r_s[...] = jnp.zeros_like(r_s)

    @pl.when(qp * Q_PAIR * Q_BLK >= k_sel)
    def _():
        sub16 = K_BLK // 16

        def count16(ref, cand):
            c16 = jnp.concatenate([cand, cand], axis=1).astype(I16)[:, None, None]
            def body(kb, acc):
                ks = pl.multiple_of(kb * K_BLK, K_BLK)
                blk = ref[:, pl.ds(ks, K_BLK), :].reshape(Q_PAIR, sub16 // 4, 4, 16, Q_BLK)
                one = jnp.ones(blk.shape, I16)
                hit = jnp.where(blk >= c16, one, jnp.zeros_like(one))
                for j in range(sub16 // 4):
                    acc = acc + hit[:, j]
                return acc
            def group(j, acc):
                for u in range(COUNT_UNROLL):
                    acc = body(COUNT_UNROLL * j + u, acc)
                return acc
            acc = lax.fori_loop(0, nkb // COUNT_UNROLL, group,
                                jnp.zeros((Q_PAIR, 4, 16, Q_BLK), I16))
            acc = lax.fori_loop(COUNT_UNROLL * (nkb // COUNT_UNROLL), nkb, body, acc)
            tot = jnp.sum(jnp.sum(acc.astype(I32), axis=1), axis=1, keepdims=True)
            return jnp.broadcast_to(tot, (Q_PAIR, 8, Q_BLK))

        def bisect16(ref, target):
            def bit_step(i, tu):
                cand_u = tu | lax.shift_left(jnp.int32(1), 15 - i)
                return jnp.where(count16(ref, cand_u - HALF16) >= target, cand_u, tu)
            return lax.fori_loop(0, 16, bit_step, jnp.zeros((Q_PAIR, 8, Q_BLK), I32))

        hi = bisect16(kh_s, k_sel) - HALF16
        need = k_sel - count16(kh_s, hi + 1)
        hi16 = jnp.concatenate([hi, hi], axis=1).astype(I16)

        def keep_equal_high(kb, carry):
            ks = pl.multiple_of(kb * K_BLK, K_BLK)
            same = kh_s[:, pl.ds(ks, K_BLK), :].reshape(Q_PAIR, sub16, 16, Q_BLK) == hi16[:, None]
            low = kl_s[:, pl.ds(ks, K_BLK), :].reshape(Q_PAIR, sub16, 16, Q_BLK)
            kl_s[:, pl.ds(ks, K_BLK), :] = jnp.where(
                same, low, jnp.full(low.shape, -HALF16, I16)).reshape(Q_PAIR, K_BLK, Q_BLK)
            return carry

        lax.fori_loop(0, nkb, keep_equal_high, 0)
        tk = hi * (2 * HALF16) + bisect16(kl_s, need)

        def count_gt(kb, acc):
            ks = pl.multiple_of(kb * K_BLK, K_BLK)
            blk = keys_s[:, pl.ds(ks, K_BLK), :].reshape(Q_PAIR, sub // 4, 4, 8, Q_BLK)
            return acc + jnp.sum(jnp.where(blk > tk[:, None, None], 1.0, 0.0), axis=1)

        acc = lax.fori_loop(0, nkb, count_gt, jnp.zeros((Q_PAIR, 4, 8, Q_BLK), F32))
        t_s[...] = tk
        r_s[...] = k_sel - jnp.broadcast_to(
            jnp.sum(jnp.sum(acc, axis=1), axis=1, keepdims=True), (Q_PAIR, 8, Q_BLK))

    m_s[...] = jnp.full(m_s.shape, NEG_BIG, F32)
    acc_s[...] = jnp.zeros_like(acc_s)
    er_s[...] = jnp.zeros_like(er_s)

    def logits(kb, p, near):
        ks = pl.multiple_of(kb * K_BLK, K_BLK)
        t8, r8 = t_s[p], r_s[p]
        key = keys_s[p, pl.ds(ks, K_BLK), :].reshape(sub, 8, Q_BLK)
        eq = key == t8
        pre = _dot(tri_ref[...], jnp.where(eq, 1.0, 0.0).reshape(K_BLK, Q_BLK).astype(BF16))
        er = er_s[p]
        take = pre.reshape(sub, 8, Q_BLK) + er <= r8
        pen_t = jnp.where(key > t8, 0.0, jnp.where(eq, jnp.where(take, 0.0, NEG_BIG), NEG_BIG))
        er_s[p] = er + jnp.broadcast_to(pre[K_BLK - 1:K_BLK, :], (8, Q_BLK))
        rhs = jnp.concatenate(
            [pen_t.reshape(K_BLK, Q_BLK).astype(BF16), kg_ref[0, pl.ds(ks, K_BLK), :]], axis=1)
        s = _nt_dot(qs_s[p], rhs)
        if near:
            s = s + bias_ref[jnp.minimum(Q_PAIR * qp + p - 2 * kb, 3)]
        return s

    def attend(kbs, near):
        for p in range(Q_PAIR):
            ss = [logits(kb, p, near) for kb in kbs]
            m_old = m_s[p]
            s_max = functools.reduce(jnp.maximum, ss)
            m_new = jnp.maximum(m_old, jnp.max(s_max, axis=1, keepdims=True))
            m2 = jnp.concatenate([m_new, m_new], axis=1)
            pr = jnp.concatenate([jnp.exp(s - m2).astype(BF16) for s in ss], axis=1)
            vg = jnp.concatenate(
                [vg_ref[0, pl.ds(pl.multiple_of(kb * K_BLK, K_BLK), K_BLK), :] for kb in kbs],
                axis=0)
            acc_s[p] = jnp.exp(m_old - m_new) * acc_s[p] + _dot(pr, vg)
            m_s[p] = m_new

    n_far = jnp.maximum(nkb - 2, 0)
    n_grp = n_far // FAR_UNROLL

    def far_group(j, carry):
        for u in range(0, FAR_UNROLL, FAR_JOINT):
            attend([FAR_UNROLL * j + u + v for v in range(FAR_JOINT)], near=False)
        return carry

    lax.fori_loop(0, n_grp, far_group, 0)
    base = FAR_UNROLL * n_grp
    left = n_far - base

    @pl.when(left >= 2)
    def _():
        attend([base, base + 1], near=False)

    @pl.when(left % 2 == 1)
    def _():
        attend([n_far - 1], near=False)

    @pl.when(nkb >= 2)
    def _():
        attend([nkb - 2, nkb - 1], near=True)

    @pl.when(nkb < 2)
    def _():
        attend([nkb - 1], near=True)

    for p in range(Q_PAIR):
        acc = acc_s[p]
        o = acc[:, 0:DSA_HEAD_DIM] / acc[:, DSA_HEAD_DIM:DSA_HEAD_DIM + 1]
        for h in range(DSA_HEADS):
            y_ref[0, rows(p), h * DSA_HEAD_DIM:(h + 1) * DSA_HEAD_DIM] = o[hrows(h)].astype(BF16)


def _dsa(main, idxf, bias, tri, eye, cfar):
    bsz, seq, _ = main.shape
    k_sel = min(DSA_TOPK_MAX, seq // 4)
    qrows = Q_PAIR * Q_BLK
    assert qrows == K_BLK and seq % K_BLK == 0 and k_sel % qrows == 0
    col = lambda c, w: c // w
    return pl.pallas_call(
        functools.partial(_dsa_kernel, k_sel=k_sel),
        grid=(bsz, seq // qrows),
        in_specs=[
            pl.BlockSpec((1, qrows, DSA_WIDTH), lambda b, q: (b, q, col(COL_DQ, DSA_WIDTH))),
            pl.BlockSpec((1, qrows, 256), lambda b, q: (b, q, col(COL_IQ, 256))),
            pl.BlockSpec((1, qrows, LANES), lambda b, q: (b, q, 0)),
            pl.BlockSpec((1, seq, LANES), lambda b, q: (b, 0, col(COL_IX, LANES))),
            pl.BlockSpec((1, seq, LANES), lambda b, q: (b, 0, col(COL_KG, LANES))),
            pl.BlockSpec((1, seq, LANES), lambda b, q: (b, 0, col(COL_VG, LANES))),
            pl.BlockSpec((4, DSA_HEADS * Q_BLK, K_BLK), lambda b, q: (0, 0, 0)),
            pl.BlockSpec((K_BLK, K_BLK), lambda b, q: (0, 0)),
            pl.BlockSpec((DSA_HEADS * Q_BLK, Q_BLK), lambda b, q: (0, 0)),
            pl.BlockSpec((DSA_HEADS, Q_BLK, DSA_HEAD_DIM), lambda b, q: (0, 0, 0)),
        ],
        out_specs=pl.BlockSpec((1, qrows, DSA_WIDTH), lambda b, q: (b, q, 0)),
        out_shape=jax.ShapeDtypeStruct((bsz, seq, DSA_WIDTH), BF16),
        scratch_shapes=[
            pltpu.VMEM((Q_PAIR, seq, Q_BLK), I32),
            pltpu.VMEM((Q_PAIR, seq, Q_BLK), I16),
            pltpu.VMEM((Q_PAIR, seq, Q_BLK), I16),
            pltpu.VMEM((Q_PAIR, DSA_HEADS * Q_BLK, 2 * Q_BLK), BF16),
            pltpu.VMEM((Q_PAIR * IDX_HEADS * Q_BLK, IDX_DIM), BF16),
            pltpu.VMEM((Q_PAIR, 8, Q_BLK), I32),
            pltpu.VMEM((Q_PAIR, 8, Q_BLK), F32),
            pltpu.VMEM((Q_PAIR, DSA_HEADS * Q_BLK, LANES), F32),
            pltpu.VMEM((Q_PAIR, DSA_HEADS * Q_BLK, LANES), F32),
            pltpu.VMEM((Q_PAIR, 8, Q_BLK), F32),
        ],
        compiler_params=pltpu.CompilerParams(
            dimension_semantics=("arbitrary", "arbitrary"), vmem_limit_bytes=VMEM_LIMIT),
        name="dsa_attention",
    )(main, main, idxf, main, main, main, bias, tri, eye, cfar)


def _ret_kernel(q_ref, k_ref, v_ref, g_ref, cos_ref, sa_ref, sb_ref, dec_ref, wq_ref, wk_ref,
                gc_ref, gn_ref, o_ref, st_ref):
    @pl.when(pl.program_id(1) == 0)
    def _():
        st_ref[...] = jnp.zeros_like(st_ref)

    cos, sa, sb = cos_ref[...], sa_ref[...], sb_ref[...]
    half = RET_HEAD_DIM // 2

    def rot(x):
        return (x * cos + pltpu.roll(x, RET_WIDTH - half, 1) * sa + pltpu.roll(x, half, 1) * sb)

    q = rot(q_ref[0].astype(F32))
    k = rot(k_ref[0].astype(F32)) * (RET_HEAD_DIM ** -0.5)
    v = v_ref[0]
    g = g_ref[0].astype(F32)
    gate = g * jax.nn.sigmoid(g)
    gn = gn_ref[...]
    for h in range(RET_HEADS):
        sl = slice(h * RET_HEAD_DIM, (h + 1) * RET_HEAD_DIM)
        qh = q[:, sl].astype(BF16)
        kf = k[:, sl]
        vh = v[:, sl]
        s = _nt_dot(qh, kf.astype(BF16)) * dec_ref[h]
        st = st_ref[h]
        o = _dot(s.astype(BF16), vh) + _dot(qh, st.astype(BF16)) * wq_ref[h]
        kw_t = (kf * wk_ref[h]).T.astype(BF16)
        st_ref[h] = gc_ref[h] * st + _dot(kw_t, vh)
        mu = jnp.mean(o, axis=-1, keepdims=True)
        d = o - mu
        var = jnp.mean(d * d, axis=-1, keepdims=True)
        on = d * lax.rsqrt(var + GN_EPS)
        o_ref[0, :, sl] = (gate[:, sl] * (on * gn[:, sl])).astype(BF16)


def _ret_tables(seq, chunk):
    half = RET_HEAD_DIM // 2
    inv = ROPE_BASE ** (-jnp.arange(half, dtype=F32) / half)
    ang = jnp.arange(seq, dtype=jnp.int32).astype(F32)[:, None] * inv[None, :]
    cos, sin = jnp.cos(ang), jnp.sin(ang)
    zero = jnp.zeros_like(sin)
    cos_t = jnp.tile(cos, (1, 2 * RET_HEADS))
    sa = jnp.tile(jnp.concatenate([-sin, zero], axis=1), (1, RET_HEADS))
    sb = jnp.tile(jnp.concatenate([zero, sin], axis=1), (1, RET_HEADS))
    gamma = 1.0 - 2.0 ** (-5.0 - jnp.arange(RET_HEADS, dtype=F32))
    log_g = jnp.log(gamma)
    i = jnp.arange(chunk, dtype=F32)
    diff = i[:, None] - i[None, :]
    decay = jnp.where(diff >= 0, jnp.exp(log_g[:, None, None] * jnp.maximum(diff, 0.0)), 0.0)
    ones = jnp.ones((1, 1, RET_HEAD_DIM), F32)
    wq = jnp.exp(log_g[:, None] * (i + 1.0)[None, :])[:, :, None] * ones
    wk = jnp.exp(log_g[:, None] * (chunk - 1.0 - i)[None, :])[:, :, None] * ones
    gc = jnp.exp(log_g * chunk)[:, None, None] * jnp.ones((1, RET_HEAD_DIM, RET_HEAD_DIM), F32)
    return cos_t, sa, sb, decay, wq, wk, gc


def _retention(main, tables, gn):
    bsz, seq, _ = main.shape
    cos_t, sa, sb, decay, wq, wk, gc = tables
    c = decay.shape[-1]
    blk = lambda col: pl.BlockSpec((1, c, RET_WIDTH), lambda b, i: (b, i, col // RET_WIDTH))
    tab = pl.BlockSpec((c, RET_WIDTH), lambda b, i: (i, 0))
    const = lambda shape: pl.BlockSpec(shape, lambda b, i: (0,) * len(shape))
    return pl.pallas_call(
        _ret_kernel,
        grid=(bsz, seq // c),
        in_specs=[
            blk(COL_RQ), blk(COL_RK), blk(COL_RV), blk(COL_RG), tab, tab, tab,
            const((RET_HEADS, c, c)), const((RET_HEADS, c, RET_HEAD_DIM)),
            const((RET_HEADS, c, RET_HEAD_DIM)),
            const((RET_HEADS, RET_HEAD_DIM, RET_HEAD_DIM)), const((1, RET_WIDTH)),
        ],
        out_specs=pl.BlockSpec((1, c, RET_WIDTH), lambda b, i: (b, i, 0)),
        out_shape=jax.ShapeDtypeStruct((bsz, seq, RET_WIDTH), BF16),
        scratch_shapes=[pltpu.VMEM((RET_HEADS, RET_HEAD_DIM, RET_HEAD_DIM), F32)],
        compiler_params=pltpu.CompilerParams(
            dimension_semantics=("arbitrary", "arbitrary"), vmem_limit_bytes=VMEM_LIMIT),
        name="retention",
    )(main, main, main, main, cos_t, sa, sb, decay, wq, wk, gc, gn)


def _ffn_kernel(x_ref, ya_ref, yb_ref, yc_ref, wo_ref, g_ref, wg_ref, wu_ref, wd_ref, gf_ref,
                o_ref, x1_s, h_s, acc_s, *, final):
    x1 = (x_ref[0]
          + _dot(ya_ref[...].astype(BF16), wo_ref[0:S5_WIDTH, :])
          + _dot(yb_ref[0], wo_ref[S5_WIDTH:S5_WIDTH + DSA_WIDTH, :])
          + _dot(yc_ref[0], wo_ref[S5_WIDTH + DSA_WIDTH:, :]))
    x1_s[...] = x1
    ms = jnp.mean(x1 * x1, axis=-1, keepdims=True)
    h_s[...] = (x1 * lax.rsqrt(ms + RMS_EPS) * g_ref[...]).astype(BF16)
    acc_s[...] = jnp.zeros_like(acc_s)

    for c in range(N_FF_CHUNK):
        cols = slice(c * FF_CHUNK, (c + 1) * FF_CHUNK)
        h = h_s[...]
        gate = _dot(h, wg_ref[:, cols])
        up = _dot(h, wu_ref[:, cols])
        a = (gate * jax.nn.sigmoid(gate) * up).astype(BF16)
        acc_s[...] += _dot(a, wd_ref[cols, :])
    y = x1_s[...] + acc_s[...]
    if final:
        ms = jnp.mean(y * y, axis=-1, keepdims=True)
        y = y * lax.rsqrt(ms + RMS_EPS) * gf_ref[...]
    o_ref[0] = y


def _ffn(x, ya, yb, yc, wo, g, wg, wu, wd, gf, final):
    bsz, seq, _ = x.shape
    tm = min(TM_FFN, seq)
    once = pl.Buffered(1)
    const = lambda shape: pl.BlockSpec(shape, lambda b, i: (0,) * len(shape), pipeline_mode=once)
    return pl.pallas_call(
        functools.partial(_ffn_kernel, final=final),
        grid=(bsz, seq // tm),
        in_specs=[
            pl.BlockSpec((1, tm, D_MODEL), lambda b, i: (b, i, 0)),
            pl.BlockSpec((tm, S5_WIDTH), lambda b, i: (i, b)),
            pl.BlockSpec((1, tm, DSA_WIDTH), lambda b, i: (b, i, 0)),
            pl.BlockSpec((1, tm, RET_WIDTH), lambda b, i: (b, i, 0)),
            const((D_MODEL, D_MODEL)), const((1, D_MODEL)),
            const((D_MODEL, D_FF)), const((D_MODEL, D_FF)),
            const((D_FF, D_MODEL)), const((1, D_MODEL)),
        ],
        out_specs=pl.BlockSpec((1, tm, D_MODEL), lambda b, i: (b, i, 0)),
        out_shape=jax.ShapeDtypeStruct((bsz, seq, D_MODEL), F32),
        scratch_shapes=[pltpu.VMEM((tm, D_MODEL), F32), pltpu.VMEM((tm, D_MODEL), BF16),
                        pltpu.VMEM((tm, D_MODEL), F32)],
        compiler_params=pltpu.CompilerParams(
            dimension_semantics=("arbitrary", "arbitrary"), vmem_limit_bytes=VMEM_LIMIT),
        name="outproj_ffn",
    )(x, ya, yb, yc, wo, g, wg, wu, wd, gf)


def _pack_w_in(w_in):
    pts = [int(s) for s in np.cumsum(IN_SPLITS)[:-1]]
    u, dq, dk, dv, iq, ik, iw, rq, rk, rv, rg = jnp.split(w_in, pts, axis=-1)
    z = lambda n: jnp.zeros(w_in.shape[:-1] + (n,), w_in.dtype)
    parts = [dq, iq, rq, rk, rv, rg, dk, z(64), dv, z(64), ik, iw, z(60), z(128), u]
    return jnp.concatenate(parts, axis=-1).astype(BF16)


def _s5_params(a_re, a_im, log_dt, b_re, b_im, c_re, c_im, d_skip):
    g, n, p = S5_GROUPS, S5_STATE, S5_GROUP
    dt = jnp.exp(log_dt)[:, None]
    mag = jnp.exp(a_re * dt)
    ang = a_im * dt
    abr, abi = mag * jnp.cos(ang), mag * jnp.sin(ang)
    den = a_re * a_re + a_im * a_im
    nr, ni = abr - 1.0, abi
    cr = (nr * a_re + ni * a_im) / den
    ci = (ni * a_re - nr * a_im) / den
    bbr = cr[..., None] * b_re - ci[..., None] * b_im
    bbi = cr[..., None] * b_im + ci[..., None] * b_re
    eye = jnp.eye(g, dtype=F32)

    def b_block(bb):
        return (jnp.swapaxes(bb, 1, 2)[:, :, None, :] * eye[:, None, :, None]).reshape(g * p, g * n)

    def c_block(cc):
        return (jnp.swapaxes(cc, 1, 2)[:, :, None, :] * eye[:, None, :, None]).reshape(g * n, g * p)

    bmat = jnp.concatenate([b_block(bbr), b_block(bbi)], axis=1).astype(BF16)
    cmat = jnp.concatenate([c_block(c_re), -c_block(c_im)], axis=0).astype(BF16)
    return bmat, cmat, abr.reshape(1, g * n), abi.reshape(1, g * n), d_skip.reshape(1, g * p)


def _tri_lower():
    j = np.arange(K_BLK)
    return jnp.asarray((j[None, :] <= j[:, None]).astype(np.float32), BF16)


def _stacked_eye():
    return jnp.asarray(np.tile(np.eye(Q_BLK, dtype=np.float32), (DSA_HEADS, 1)), BF16)


def _far_bias_cols(rel_bias):
    c = rel_bias[REL_BUCKETS - 1, :].astype(F32)
    hi = c.astype(BF16)
    lo = (c - hi.astype(F32)).astype(BF16)
    cols = jnp.zeros((DSA_HEADS, Q_BLK, DSA_HEAD_DIM), BF16)
    cols = cols.at[:, :, 0].set(hi[:, None])
    return cols.at[:, :, 1].set(lo[:, None])


def kernel(x, w_in, w_out, norm_mix, ssm_a_re, ssm_a_im, ssm_log_dt, ssm_b_re, ssm_b_im,
           ssm_c_re, ssm_c_im, ssm_d, ssm_glu_w, ssm_glu_b, rel_bias, ret_gn, norm_ffn,
           w_ffn_in, w_ffn_out, norm_final):
    bsz, seq, _ = x.shape
    depth = w_in.shape[0]
    w_all = _pack_w_in(w_in)
    wo = w_out.astype(BF16)
    wg = w_ffn_in[:, :, :D_FF].astype(BF16)
    wu = w_ffn_in[:, :, D_FF:].astype(BF16)
    wd = w_ffn_out.astype(BF16)
    glu_w = ssm_glu_w.astype(BF16)

    bias = _bias_tiles(rel_bias)
    tri = _tri_lower()
    eye = _stacked_eye()
    cfar = _far_bias_cols(rel_bias)
    tables = _ret_tables(seq, min(RET_CHUNK, seq))
    s5p_all = jax.vmap(_s5_params)(ssm_a_re, ssm_a_im, ssm_log_dt, ssm_b_re, ssm_b_im,
                                   ssm_c_re, ssm_c_im, ssm_d)

    for l in range(depth):
        main, idxf, u_t = _inproj(x, norm_mix[l][None, :], w_all[l])
        s5p = [a[l] for a in s5p_all]
        ya = _s5(u_t.reshape(seq, bsz, S5_WIDTH), *s5p, glu_w[l], ssm_glu_b[l][None, :])
        yb = _dsa(main, idxf, bias, tri, eye, cfar)
        yc = _retention(main, tables, ret_gn[l][None, :])
        x = _ffn(x, ya.reshape(seq, bsz * S5_WIDTH), yb, yc, wo[l], norm_ffn[l][None, :],
                 wg[l], wu[l], wd[l], norm_final[None, :], final=(l == depth - 1))
    return x
```
